```python
import math
import jax, jax.numpy as jnp
from jax import lax
import numpy as np

D_MODEL = 1024
BATCH = 2
SEQ = 16384
DEPTH = 1
DEC_BATCH = 8
DEC_SEQ = 4096
PAST_LEN = 128

GRID_W = 64
CONV_CH = D_MODEL // 2
CONV_K = 31
NA_HEADS = 16
NA_HEAD_DIM = 32
NA_WIDTH = NA_HEADS * NA_HEAD_DIM
WIN_R_MAX = 8
WIN_C = 16
OFF_CONV = 0
OFF_Q = OFF_CONV + 2 * CONV_CH
OFF_K = OFF_Q + NA_WIDTH
OFF_V = OFF_K + NA_WIDTH
OFF_GATE = OFF_V + NA_WIDTH
IN_COLS = OFF_GATE + 2 * D_MODEL
N_EXPERTS = 64
TOP_K = 8
N_GROUP = 8
TOPK_GROUP = 4
EXPERT_FF = 256
SHARED_FF = 256
ROUTED_SCALE = 2.5
EXPERT_BLOCK = 128
PLE_DIM = 256
EPS = 1e-6

kernel_name = "hybrid_conformer_natten_moe_encoder"


def rms_norm(x, g):
    xf = x.astype(jnp.float32)
    y = xf * lax.rsqrt(jnp.mean(xf * xf, axis=-1, keepdims=True) + EPS)
    return (y * g.astype(jnp.float32)).astype(x.dtype)


def layer_norm(x, g, b):
    xf = x.astype(jnp.float32)
    mu = jnp.mean(xf, axis=-1, keepdims=True)
    var = jnp.mean(jnp.square(xf - mu), axis=-1, keepdims=True)
    y = (xf - mu) * lax.rsqrt(var + EPS)
    return (y * g.astype(jnp.float32) + b.astype(jnp.float32)).astype(x.dtype)


def conformer_conv(a, w_dw, b_dw, ln_g, ln_b, w_pw):
    u, gate = jnp.split(a, 2, axis=-1)
    y = u * jax.nn.sigmoid(gate)
    y = lax.conv_general_dilated(
        y, w_dw[:, None, :].astype(y.dtype), window_strides=(1,),
        padding=[(CONV_K // 2, CONV_K // 2)],
        dimension_numbers=('NWC', 'WIO', 'NWC'),
        feature_group_count=CONV_CH) + b_dw
    y = jax.nn.silu(layer_norm(y, ln_g, ln_b))
    return y @ w_pw


def neighborhood_attention(q, k, v, rpb):
    bsz, seq = q.shape[0], q.shape[1]
    rows = seq // GRID_W
    wr = min(WIN_R_MAX, rows)
    scale = NA_HEAD_DIM ** -0.5
    qg = q.reshape(bsz, rows, GRID_W, NA_HEADS, NA_HEAD_DIM)
    kg = k.reshape(bsz, rows, GRID_W, NA_HEADS, NA_HEAD_DIM)
    vg = v.reshape(bsz, rows, GRID_W, NA_HEADS, NA_HEAD_DIM)
    col = np.arange(GRID_W)
    col_start = np.clip(col - WIN_C // 2, 0, GRID_W - WIN_C)
    col_idx = col_start[:, None] + np.arange(WIN_C)[None, :]
    dc = col_idx - col[:, None] + (WIN_C - 1)

    def one_row(args):
        r, q_row = args
        r0 = jnp.clip(r - wr // 2, 0, rows - wr)
        k_band = lax.dynamic_slice_in_dim(kg, r0, wr, axis=1)
        v_band = lax.dynamic_slice_in_dim(vg, r0, wr, axis=1)
        k_sel = k_band[:, :, col_idx]
        v_sel = v_band[:, :, col_idx]
        s = jnp.einsum('bchd,brcjhd->bhcrj', q_row * scale, k_sel).astype(jnp.float32)
        dr = r0 + jnp.arange(wr) - r + (WIN_R_MAX - 1)
        bias = rpb[:, dr[:, None, None], dc[None, :, :]]
        s = s + jnp.transpose(bias, (0, 2, 1, 3))[None].astype(jnp.float32)
        shp = s.shape
        pr = jax.nn.softmax(s.reshape(shp[:3] + (wr * WIN_C,)), axis=-1).reshape(shp)
        return jnp.einsum('bhcrj,brcjhd->bchd', pr.astype(v.dtype), v_sel)

    out = lax.map(one_row, (jnp.arange(rows), jnp.transpose(qg, (1, 0, 2, 3, 4))))
    return jnp.transpose(out, (1, 0, 2, 3, 4)).reshape(bsz, seq, NA_WIDTH)


def swiglu(h, w_gu, w_down):
    g, u = jnp.split(h @ w_gu, 2, axis=-1)
    return (jax.nn.silu(g) * u) @ w_down


def routed_experts(h, top_idx, top_w, w_gu, w_down):
    n, d = h.shape
    n_assign = n * TOP_K
    flat_e = top_idx.reshape(-1)
    order = jnp.argsort(flat_e)
    sorted_e = flat_e[order]
    tok = order // TOP_K
    w_sorted = top_w.reshape(-1)[order]
    counts = jnp.bincount(flat_e, length=N_EXPERTS)
    starts = jnp.cumsum(counts) - counts
    padded_counts = ((counts + EXPERT_BLOCK - 1) // EXPERT_BLOCK) * EXPERT_BLOCK
    padded_ends = jnp.cumsum(padded_counts)
    padded_starts = padded_ends - padded_counts
    dest = padded_starts[sorted_e] + (jnp.arange(n_assign) - starts[sorted_e])
    n_blocks = -(-n_assign // EXPERT_BLOCK) + N_EXPERTS
    n_rows = n_blocks * EXPERT_BLOCK
    row_tok = jnp.full((n_rows,), n, dtype=jnp.int32).at[dest].set(tok.astype(jnp.int32))
    row_w = jnp.zeros((n_rows,), h.dtype).at[dest].set(w_sorted.astype(h.dtype))
    block_expert = jnp.minimum(
        jnp.searchsorted(padded_ends, jnp.arange(n_blocks) * EXPERT_BLOCK, side='right'),
        N_EXPERTS - 1)
    h_pad = jnp.concatenate([h, jnp.zeros((1, d), h.dtype)], axis=0)

    def step(acc, blk):
        e, idx, w = blk
        y = swiglu(h_pad[idx], w_gu[e], w_down[e])
        return acc.at[idx].add(y * w[:, None]), None

    acc, _ = lax.scan(step, jnp.zeros((n + 1, d), h.dtype),
                      (block_expert, row_tok.reshape(n_blocks, EXPERT_BLOCK),
                       row_w.reshape(n_blocks, EXPERT_BLOCK)))
    return acc[:n]


def moe(h, w_router, router_bias, w_exp_gu, w_exp_down, w_sh_gu, w_sh_down):
    n = h.shape[0]
    scores = jax.nn.sigmoid(h.astype(jnp.float32) @ w_router.astype(jnp.float32))
    biased = scores + router_bias.astype(jnp.float32)
    per_group = N_EXPERTS // N_GROUP
    grp_score = lax.top_k(biased.reshape(n, N_GROUP, per_group), 2)[0].sum(-1)
    _, grp_idx = lax.top_k(grp_score, TOPK_GROUP)
    grp_mask = jax.nn.one_hot(grp_idx, N_GROUP, dtype=jnp.float32).sum(1) > 0
    masked = jnp.where(jnp.repeat(grp_mask, per_group, axis=1), biased, -jnp.inf)
    _, top_idx = lax.top_k(masked, TOP_K)
    top_w = jnp.take_along_axis(scores, top_idx, axis=1)
    top_w = top_w / jnp.sum(top_w, axis=-1, keepdims=True) * ROUTED_SCALE
    routed = routed_experts(h, top_idx, top_w, w_exp_gu, w_exp_down)
    return routed + swiglu(h, w_sh_gu, w_sh_down)


def encoder_layer(x, p, g_mix, w_in, b_in, conv_w, conv_b, conv_ln_g, conv_ln_b, w_conv_out,
                  rpb, w_attn_out, w_out, g_ffn, w_router, router_bias, w_exp_gu, w_exp_down,
                  w_sh_gu, w_sh_down, g_ple, w_ple_gate, w_ple_proj):
    bsz, seq, d = x.shape
    h = rms_norm(x, g_mix)
    z = h @ w_in + b_in
    y_conv = conformer_conv(z[..., OFF_CONV:OFF_Q], conv_w, conv_b, conv_ln_g, conv_ln_b, w_conv_out)
    hs = (bsz, seq, NA_HEADS, NA_HEAD_DIM)
    y_na = neighborhood_attention(z[..., OFF_Q:OFF_K].reshape(hs), z[..., OFF_K:OFF_V].reshape(hs),
                                  z[..., OFF_V:OFF_GATE].reshape(hs), rpb) @ w_attn_out
    g_conv, g_na = jnp.split(z[..., OFF_GATE:], 2, axis=-1)
    x = x + (jax.nn.sigmoid(g_conv) * y_conv + jax.nn.sigmoid(g_na) * y_na) @ w_out
    h2 = rms_norm(x, g_ffn).reshape(bsz * seq, d)
    x = x + moe(h2, w_router, router_bias, w_exp_gu, w_exp_down, w_sh_gu, w_sh_down).reshape(bsz, seq, d)
    h3 = rms_norm(x, g_ple)
    x = x + (p @ w_ple_proj) * jax.nn.sigmoid(h3 @ w_ple_gate)
    return x


def setup_inputs(seed: int = 0) -> dict:
    key = jax.random.key(seed)
    ks = jax.random.split(key, 32)
    f32 = jnp.float32
    nrm = lambda k, shape, s: jax.random.normal(k, shape, f32) * s
    L, D = DEPTH, D_MODEL
    return {
        "x_prompt": nrm(ks[0], (BATCH, SEQ, D), 1.0),
        "x_sample": nrm(ks[1], (DEC_BATCH, DEC_SEQ, D), 1.0),
        "p_prompt": nrm(ks[2], (DEPTH, BATCH, SEQ, PLE_DIM), 1.0),
        "p_sample": nrm(ks[3], (DEPTH, DEC_BATCH, DEC_SEQ, PLE_DIM), 1.0),
        "g_mix": 1.0 + nrm(ks[4], (L, D), 0.05),
        "w_in": nrm(ks[5], (L, D, IN_COLS), D ** -0.5),
        "b_in": nrm(ks[6], (L, IN_COLS), 0.02),
        "conv_w": nrm(ks[7], (L, CONV_K, CONV_CH), CONV_K ** -0.5),
        "conv_b": nrm(ks[8], (L, CONV_CH), 0.02),
        "conv_ln_g": 1.0 + nrm(ks[9], (L, CONV_CH), 0.05),
        "conv_ln_b": nrm(ks[10], (L, CONV_CH), 0.02),
        "w_conv_out": nrm(ks[11], (L, CONV_CH, D), CONV_CH ** -0.5),
        "rpb": nrm(ks[12], (L, NA_HEADS, 2 * WIN_R_MAX - 1, 2 * WIN_C - 1), 0.1),
        "w_attn_out": nrm(ks[13], (L, NA_WIDTH, D), NA_WIDTH ** -0.5),
        "w_out": nrm(ks[14], (L, D, D), D ** -0.5),
        "g_ffn": 1.0 + nrm(ks[15], (L, D), 0.05),
        "w_router": nrm(ks[16], (L, D, N_EXPERTS), D ** -0.5),
        "router_bias": nrm(ks[17], (L, N_EXPERTS), 0.01),
        "w_exp_gu": nrm(ks[18], (L, N_EXPERTS, D, 2 * EXPERT_FF), D ** -0.5),
        "w_exp_down": nrm(ks[19], (L, N_EXPERTS, EXPERT_FF, D), EXPERT_FF ** -0.5),
        "w_sh_gu": nrm(ks[20], (L, D, 2 * SHARED_FF), D ** -0.5),
        "w_sh_down": nrm(ks[21], (L, SHARED_FF, D), SHARED_FF ** -0.5),
        "g_ple": 1.0 + nrm(ks[22], (L, D), 0.05),
        "w_ple_gate": nrm(ks[23], (L, D, D), D ** -0.5),
        "w_ple_proj": nrm(ks[24], (L, PLE_DIM, D), PLE_DIM ** -0.5),
        "g_final": 1.0 + nrm(ks[25], (D,), 0.05),
    }


def reference(x_prompt, x_sample, p_prompt, p_sample, g_mix, w_in, b_in, conv_w, conv_b,
              conv_ln_g, conv_ln_b, w_conv_out, rpb, w_attn_out, w_out, g_ffn, w_router,
              router_bias, w_exp_gu, w_exp_down, w_sh_gu, w_sh_down, g_ple, w_ple_gate,
              w_ple_proj, g_final):
    xp = x_prompt
    xs = x_sample
    for i in range(DEPTH):
        lp = (g_mix[i], w_in[i], b_in[i], conv_w[i], conv_b[i], conv_ln_g[i], conv_ln_b[i],
              w_conv_out[i], rpb[i], w_attn_out[i], w_out[i], g_ffn[i], w_router[i],
              router_bias[i], w_exp_gu[i], w_exp_down[i], w_sh_gu[i], w_sh_down[i], g_ple[i],
              w_ple_gate[i], w_ple_proj[i])
        xp = encoder_layer(xp, p_prompt[i], *lp)
        xs = encoder_layer(xs, p_sample[i], *lp)
    y_prompt = rms_norm(xp, g_final)
    y_sample = rms_norm(xs, g_final)
    return (y_prompt, y_sample)
```

```python
import functools
import math

import jax
import jax.numpy as jnp
import numpy as np
from jax import lax
from jax.experimental import pallas as pl
from jax.experimental.pallas import tpu as pltpu

D_MODEL = 1024
GRID_W = 64
CONV_CH = 512
CONV_K = 31
NA_HEADS = 16
NA_HEAD_DIM = 32
NA_WIDTH = NA_HEADS * NA_HEAD_DIM
WIN_R = 8
WIN_C = 16
N_EXPERTS = 64
TOP_K = 8
N_GROUP = 8
TOPK_GROUP = 4
EXPERT_FF = 256
SHARED_FF = 256
ROUTED_SCALE = 2.5
PLE_DIM = 256
EPS = 1e-6
NEG_BIG = -1e30

OFF_Q = 2 * CONV_CH
OFF_K = OFF_Q + NA_WIDTH
OFF_V = OFF_K + NA_WIDTH
OFF_GATE = OFF_V + NA_WIDTH

CONV_HALO = 16
VMEM_LIMIT = 56 * 1024 * 1024

BF16 = jnp.bfloat16
F32 = jnp.float32


def _cparams(sem):
    return pltpu.CompilerParams(dimension_semantics=sem, vmem_limit_bytes=VMEM_LIMIT)


def _const_spec(shape):
    nd = len(shape)
    return pl.BlockSpec(shape, lambda *_: (0,) * nd)


def _rms(xf, g):
    return xf * lax.rsqrt(jnp.mean(xf * xf, axis=-1, keepdims=True) + EPS) * g


def _sigmoid(x):
    return 1.0 / (1.0 + jnp.exp(-x))


def _in_proj_kernel(x_ref, g_ref, w_ref, b_ref, glu_ref, q_ref, k_ref, v_ref, sgc_ref, sgn_ref):
    h = _rms(x_ref[...], g_ref[...]).astype(BF16)

    def proj(lo, hi):
        return jnp.dot(h, w_ref[:, lo:hi], preferred_element_type=F32) + b_ref[:, lo:hi]

    u = proj(0, CONV_CH)
    gate = proj(CONV_CH, 2 * CONV_CH)
    glu_ref[...] = (u * _sigmoid(gate)).astype(BF16)
    q_ref[...] = (proj(OFF_Q, OFF_K) * (NA_HEAD_DIM ** -0.5)).astype(BF16)
    k_ref[...] = proj(OFF_K, OFF_V).astype(BF16)
    v_ref[...] = proj(OFF_V, OFF_GATE).astype(BF16)
    for c in range(2):
        lo = OFF_GATE + c * CONV_CH
        sgc_ref[:, c * CONV_CH:(c + 1) * CONV_CH] = _sigmoid(proj(lo, lo + CONV_CH)).astype(BF16)
        lo = OFF_GATE + D_MODEL + c * CONV_CH
        sgn_ref[:, c * CONV_CH:(c + 1) * CONV_CH] = _sigmoid(proj(lo, lo + CONV_CH)).astype(BF16)


def _in_proj(x, g_mix, w_in, b_in, tm):
    n = x.shape[0]
    cols = w_in.shape[1]
    row = lambda w: pl.BlockSpec((tm, w), lambda i: (i, 0))
    out_w = (CONV_CH, NA_WIDTH, NA_WIDTH, NA_WIDTH, D_MODEL, D_MODEL)
    return pl.pallas_call(
        _in_proj_kernel,
        grid=(n // tm,),
        in_specs=[row(D_MODEL), _const_spec((1, D_MODEL)), _const_spec((D_MODEL, cols)),
                  _const_spec((1, cols))],
        out_specs=[row(w) for w in out_w],
        out_shape=[jax.ShapeDtypeStruct((n, w), BF16) for w in out_w],
        compiler_params=_cparams(("parallel",)),
        name="in_proj",
    )(x, g_mix, w_in, b_in)


def _conv_kernel(prev_ref, cur_ref, next_ref, w_ref, b_ref, lg_ref, lb_ref, o_ref, ext_ref, *, ts):
    i = pl.program_id(1)
    last = pl.num_programs(1) - 1
    prev = prev_ref[0].astype(F32)
    nxt = next_ref[0].astype(F32)
    ext_ref[0:CONV_HALO, :] = jnp.where(i > 0, prev, 0.0)
    ext_ref[CONV_HALO:CONV_HALO + ts, :] = cur_ref[0].astype(F32)
    ext_ref[CONV_HALO + ts:, :] = jnp.where(i < last, nxt, 0.0)
    base = CONV_HALO - CONV_K // 2
    acc = jnp.zeros((ts, CONV_CH), F32)
    for j in range(CONV_K):
        acc = acc + ext_ref[base + j:base + j + ts, :] * w_ref[j:j + 1, :]
    y = acc + b_ref[...]
    mu = jnp.mean(y, axis=-1, keepdims=True)
    yc = y - mu
    var = jnp.mean(yc * yc, axis=-1, keepdims=True)
    z = yc * lax.rsqrt(var + EPS) * lg_ref[...] + lb_ref[...]
    o_ref[0] = (z * _sigmoid(z)).astype(BF16)


def _conv_branch(glu, conv_w, conv_b, ln_g, ln_b, ts):
    bsz, seq, ch = glu.shape
    nh = ts // CONV_HALO
    n_halo = seq // CONV_HALO
    return pl.pallas_call(
        functools.partial(_conv_kernel, ts=ts),
        grid=(bsz, seq // ts),
        in_specs=[
            pl.BlockSpec((1, CONV_HALO, ch), lambda b, i: (b, jnp.maximum(i * nh - 1, 0), 0)),
            pl.BlockSpec((1, ts, ch), lambda b, i: (b, i, 0)),
            pl.BlockSpec((1, CONV_HALO, ch), lambda b, i: (b, jnp.minimum((i + 1) * nh, n_halo - 1), 0)),
            _const_spec((CONV_K, ch)), _const_spec((1, ch)), _const_spec((1, ch)), _const_spec((1, ch)),
        ],
        out_specs=pl.BlockSpec((1, ts, ch), lambda b, i: (b, i, 0)),
        out_shape=jax.ShapeDtypeStruct((bsz, seq, ch), BF16),
        scratch_shapes=[pltpu.VMEM((ts + 2 * CONV_HALO, ch), F32)],
        compiler_params=_cparams(("parallel", "parallel")),
        name="conv_branch",
    )(glu, glu, glu, conv_w, conv_b, ln_g, ln_b)


def _na_bias_table(rpb):
    col = np.arange(GRID_W)
    col_start = np.clip(col - WIN_C // 2, 0, GRID_W - WIN_C)
    kc = np.arange(GRID_W)
    inside = (kc[None, :] >= col_start[:, None]) & (kc[None, :] < col_start[:, None] + WIN_C)
    dc = np.clip(kc[None, :] - col[:, None] + (WIN_C - 1), 0, 2 * WIN_C - 2)
    dr = np.arange(WIN_R)[:, None] + np.arange(WIN_R)[None, :]
    t = rpb[:, dr[:, :, None, None], dc[None, None, :, :]]
    t = jnp.where(inside[None, None, None], t, NEG_BIG)
    t = jnp.transpose(t, (1, 0, 3, 2, 4))
    return t.reshape(WIN_R, NA_HEADS, GRID_W, WIN_R * GRID_W).astype(F32)


def _na_kernel(q_ref, kp_ref, kc_ref, kn_ref, vp_ref, vc_ref, vn_ref, tab_ref, o_ref,
               kband_ref, vband_ref, *, rows, rb_rows):
    blk = rb_rows * GRID_W
    rb = pl.program_id(1)
    kband_ref[0:blk, :] = kp_ref[0]
    kband_ref[blk:2 * blk, :] = kc_ref[0]
    kband_ref[2 * blk:, :] = kn_ref[0]
    vband_ref[0:blk, :] = vp_ref[0]
    vband_ref[blk:2 * blk, :] = vc_ref[0]
    vband_ref[2 * blk:, :] = vn_ref[0]
    band = WIN_R * GRID_W

    def row_body(i, carry):
        r = rb * rb_rows + i
        r0 = jnp.clip(r - WIN_R // 2, 0, rows - WIN_R)
        start = pl.multiple_of((r0 - (rb - 1) * rb_rows) * GRID_W, GRID_W)
        o = r0 - r + (WIN_R - 1)
        kb = kband_ref[pl.ds(start, band), :]
        vb = vband_ref[pl.ds(start, band), :]
        qoff = pl.multiple_of(i * GRID_W, GRID_W)
        qi = q_ref[0, pl.ds(qoff, GRID_W), :]
        outs = []
        for h in range(NA_HEADS):
            sl = slice(h * NA_HEAD_DIM, (h + 1) * NA_HEAD_DIM)
            s = lax.dot_general(qi[:, sl], kb[:, sl], (((1,), (1,)), ((), ())),
                                preferred_element_type=F32)
            s = s + tab_ref[o, h]
            m = jnp.max(s, axis=-1, keepdims=True)
            p = jnp.exp(s - m)
            l = jnp.sum(p, axis=-1, keepdims=True)
            oh = jnp.dot(p.astype(BF16), vb[:, sl], preferred_element_type=F32)
            outs.append(oh / l)
        o_ref[0, pl.ds(qoff, GRID_W), :] = jnp.concatenate(outs, axis=-1).astype(BF16)
        return carry

    lax.fori_loop(0, rb_rows, row_body, 0)


def _natten(q, k, v, tab, rb_rows):
    bsz, seq, width = q.shape
    rows = seq // GRID_W
    nrb = rows // rb_rows
    blk = rb_rows * GRID_W
    cur = pl.BlockSpec((1, blk, width), lambda b, i: (b, i, 0))
    prv = pl.BlockSpec((1, blk, width), lambda b, i: (b, jnp.maximum(i - 1, 0), 0))
    nxt = pl.BlockSpec((1, blk, width), lambda b, i: (b, jnp.minimum(i + 1, nrb - 1), 0))
    return pl.pallas_call(
        functools.partial(_na_kernel, rows=rows, rb_rows=rb_rows),
        grid=(bsz, nrb),
        in_specs=[cur, prv, cur, nxt, prv, cur, nxt, _const_spec(tab.shape)],
        out_specs=cur,
        out_shape=jax.ShapeDtypeStruct((bsz, seq, width), BF16),
        scratch_shapes=[pltpu.VMEM((3 * blk, width), BF16), pltpu.VMEM((3 * blk, width), BF16)],
        compiler_params=_cparams(("parallel", "parallel")),
        name="natten",
    )(q, k, k, k, v, v, v, tab)


def _route(logits_t, bias_col):
    e, t = logits_t.shape
    per_group = e // N_GROUP
    scores = _sigmoid(logits_t)
    biased = scores + bias_col
    b3 = biased.reshape(N_GROUP, per_group, t)
    idx3 = lax.broadcasted_iota(jnp.int32, b3.shape, 1)
    m1 = jnp.max(b3, axis=1, keepdims=True)
    first = jnp.min(jnp.where(b3 == m1, idx3, per_group), axis=1, keepdims=True)
    m2 = jnp.max(jnp.where(idx3 == first, -jnp.inf, b3), axis=1, keepdims=True)
    gscore = (m1 + m2).reshape(N_GROUP, t)
    gidx = lax.broadcasted_iota(jnp.int32, gscore.shape, 0)
    grank = jnp.zeros(gscore.shape, jnp.int32)
    for g in range(N_GROUP):
        other = gscore[g:g + 1, :]
        beats = (other > gscore) | ((other == gscore) & (g < gidx))
        grank = grank + beats.astype(jnp.int32)
    gsel = (grank < TOPK_GROUP).reshape(N_GROUP, 1, t)
    masked = jnp.where(gsel, b3, -jnp.inf).reshape(e, t)
    eidx = lax.broadcasted_iota(jnp.int32, masked.shape, 0)
    rank = jnp.zeros(masked.shape, jnp.int32)
    for j in range(e):
        other = masked[j:j + 1, :]
        beats = (other > masked) | ((other == masked) & (j < eidx))
        rank = rank + beats.astype(jnp.int32)
    sel = rank < TOP_K
    w = jnp.where(sel, scores, 0.0)
    return w / jnp.sum(w, axis=0, keepdims=True) * ROUTED_SCALE


def _mix_kernel(x_ref, ca_ref, na_ref, sgc_ref, sgn_ref, wco_ref, wao_ref, wout_ref, g_ref,
                wr_ref, rb_ref, x1_ref, h2_ref, cw_ref):
    yc = jnp.dot(ca_ref[...], wco_ref[...], preferred_element_type=F32)
    yn = jnp.dot(na_ref[...], wao_ref[...], preferred_element_type=F32)
    m = sgc_ref[...].astype(F32) * yc + sgn_ref[...].astype(F32) * yn
    x1 = x_ref[...] + jnp.dot(m.astype(BF16), wout_ref[...], preferred_element_type=F32)
    x1_ref[...] = x1
    h2 = _rms(x1, g_ref[...])
    h2_ref[...] = h2.astype(BF16)
    logits_t = lax.dot_general(wr_ref[...], h2, (((1,), (1,)), ((), ())),
                               preferred_element_type=F32, precision=lax.Precision.HIGHEST)
    cw_ref[...] = _route(logits_t, rb_ref[...])


def _mix_route(x, ca, na, sgc, sgn, wco, wao, wout, g_ffn, w_router_t, router_bias, tm):
    n = x.shape[0]
    row = lambda w: pl.BlockSpec((tm, w), lambda i: (i, 0))
    return pl.pallas_call(
        _mix_kernel,
        grid=(n // tm,),
        in_specs=[row(D_MODEL), row(CONV_CH), row(NA_WIDTH), row(D_MODEL), row(D_MODEL),
                  _const_spec(wco.shape), _const_spec(wao.shape), _const_spec(wout.shape),
                  _const_spec((1, D_MODEL)), _const_spec(w_router_t.shape), _const_spec((N_EXPERTS, 1))],
        out_specs=[row(D_MODEL), row(D_MODEL), pl.BlockSpec((N_EXPERTS, tm), lambda i: (0, i))],
        out_shape=[jax.ShapeDtypeStruct((n, D_MODEL), F32), jax.ShapeDtypeStruct((n, D_MODEL), BF16),
                   jax.ShapeDtypeStruct((N_EXPERTS, n), F32)],
        compiler_params=_cparams(("parallel",)),
        name="mix_route",
    )(x, ca, na, sgc, sgn, wco, wao, wout, g_ffn, w_router_t, router_bias)


def _moe_kernel(h_ref, cw_ref, wgu_ref, wd_ref, y_ref, acc_ref):
    e = pl.program_id(1)

    @pl.when(e == 0)
    def _():
        acc_ref[...] = jnp.zeros_like(acc_ref)

    gu_t = lax.dot_general(wgu_ref[0], h_ref[...], (((1,), (1,)), ((), ())),
                           preferred_element_type=F32)
    g_t = gu_t[:EXPERT_FF]
    u_t = gu_t[EXPERT_FF:]
    act_t = g_t * _sigmoid(g_t) * u_t * cw_ref[0]
    acc_ref[...] += jnp.dot(wd_ref[0], act_t.astype(BF16), preferred_element_type=F32)

    @pl.when(e == pl.num_programs(1) - 1)
    def _():
        y_ref[...] = acc_ref[...].T


def _moe(h2, cw, wgu_t, wd_t, tm):
    n = h2.shape[0]
    ne = wgu_t.shape[0]
    cw3 = cw.reshape(ne, 1, n)
    return pl.pallas_call(
        _moe_kernel,
        grid=(n // tm, ne),
        in_specs=[pl.BlockSpec((tm, D_MODEL), lambda i, e: (i, 0)),
                  pl.BlockSpec((1, 1, tm), lambda i, e: (e, 0, i)),
                  pl.BlockSpec((1,) + wgu_t.shape[1:], lambda i, e: (e, 0, 0)),
                  pl.BlockSpec((1,) + wd_t.shape[1:], lambda i, e: (e, 0, 0))],
        out_specs=pl.BlockSpec((tm, D_MODEL), lambda i, e: (i, 0)),
        out_shape=jax.ShapeDtypeStruct((n, D_MODEL), F32),
        scratch_shapes=[pltpu.VMEM((D_MODEL, tm), F32)],
        compiler_params=_cparams(("parallel", "arbitrary")),
        name="moe_experts",
    )(h2, cw3, wgu_t, wd_t)


def _tail_kernel(x1_ref, y_ref, h2_ref, p_ref, wsg_ref, wsd_ref, gp_ref, wpg_ref, wpp_ref, gf_ref, o_ref, *,
                 final_norm):
    gu = jnp.dot(h2_ref[...], wsg_ref[...], preferred_element_type=F32)
    g = gu[:, :SHARED_FF]
    act = g * _sigmoid(g) * gu[:, SHARED_FF:]
    sh = jnp.dot(act.astype(BF16), wsd_ref[...], preferred_element_type=F32)
    x2 = x1_ref[...] + y_ref[...] + sh
    h3 = _rms(x2, gp_ref[...]).astype(BF16)
    gate = _sigmoid(jnp.dot(h3, wpg_ref[...], preferred_element_type=F32))
    pp = jnp.dot(p_ref[...].astype(BF16), wpp_ref[...], preferred_element_type=F32)
    x3 = x2 + pp * gate
    o_ref[...] = _rms(x3, gf_ref[...]) if final_norm else x3


def _tail(x1, y, h2, p, wsg, wsd, g_ple, wpg, wpp, g_final, tm, final_norm):
    n = x1.shape[0]
    row = lambda w: pl.BlockSpec((tm, w), lambda i: (i, 0))
    return pl.pallas_call(
        functools.partial(_tail_kernel, final_norm=final_norm),
        grid=(n // tm,),
        in_specs=[row(D_MODEL), row(D_MODEL), row(D_MODEL), row(PLE_DIM),
                  _const_spec(wsg.shape), _const_spec(wsd.shape), _const_spec((1, D_MODEL)),
                  _const_spec(wpg.shape), _const_spec(wpp.shape), _const_spec((1, D_MODEL))],
        out_specs=row(D_MODEL),
        out_shape=jax.ShapeDtypeStruct((n, D_MODEL), F32),
        compiler_params=_cparams(("parallel",)),
        name="tail",
    )(x1, y, h2, p, wsg, wsd, g_ple, wpg, wpp, g_final)


def _tile(n, pref):
    t = min(pref, n)
    while n % t:
        t //= 2
    return t


def _encoder_group(x, p, w, final_norm):
    bsz, seq, d = x.shape
    n = bsz * seq
    xf = x.reshape(n, d)
    tm = _tile(n, 512)
    glu, q, k, v, sgc, sgn = _in_proj(xf, w["g_mix"], w["w_in"], w["b_in"], tm)
    ca = _conv_branch(glu.reshape(bsz, seq, CONV_CH), w["conv_w"], w["conv_b"], w["conv_ln_g"],
                      w["conv_ln_b"], _tile(seq, 512))
    b3 = lambda a: a.reshape(bsz, seq, NA_WIDTH)
    na = _natten(b3(q), b3(k), b3(v), w["na_tab"], _tile(seq // GRID_W, 8))
    x1, h2, cw = _mix_route(xf, ca.reshape(n, CONV_CH), na.reshape(n, NA_WIDTH), sgc, sgn,
                            w["w_conv_out"], w["w_attn_out"], w["w_out"], w["g_ffn"],
                            w["w_router_t"], w["router_bias"], tm)
    y = _moe(h2, cw, w["w_exp_gu_t"], w["w_exp_down_t"], _tile(n, 1024))
    out = _tail(x1, y, h2, p.reshape(n, PLE_DIM), w["w_sh_gu"], w["w_sh_down"], w["g_ple"],
                w["w_ple_gate"], w["w_ple_proj"], w["g_final"], tm, final_norm)
    return out.reshape(bsz, seq, d)


def kernel(x_prompt, x_sample, p_prompt, p_sample, g_mix, w_in, b_in, conv_w, conv_b, conv_ln_g,
           conv_ln_b, w_conv_out, rpb, w_attn_out, w_out, g_ffn, w_router, router_bias, w_exp_gu,
           w_exp_down, w_sh_gu, w_sh_down, g_ple, w_ple_gate, w_ple_proj, g_final):
    depth = g_mix.shape[0]
    xp, xs = x_prompt, x_sample
    for i in range(depth):
        row = lambda a: a[i].reshape(1, -1).astype(F32)
        w = {
            "g_mix": row(g_mix), "w_in": w_in[i].astype(BF16), "b_in": row(b_in),
            "conv_w": conv_w[i].astype(F32), "conv_b": row(conv_b),
            "conv_ln_g": row(conv_ln_g), "conv_ln_b": row(conv_ln_b),
            "w_conv_out": w_conv_out[i].astype(BF16), "na_tab": _na_bias_table(rpb[i]),
            "w_attn_out": w_attn_out[i].astype(BF16), "w_out": w_out[i].astype(BF16),
            "g_ffn": row(g_ffn), "w_router_t": w_router[i].T.astype(F32),
            "router_bias": router_bias[i].reshape(-1, 1).astype(F32),
            "w_exp_gu_t": jnp.swapaxes(w_exp_gu[i], 1, 2).astype(BF16),
            "w_exp_down_t": jnp.swapaxes(w_exp_down[i], 1, 2).astype(BF16),
            "w_sh_gu": w_sh_gu[i].astype(BF16), "w_sh_down": w_sh_down[i].astype(BF16),
            "g_ple": row(g_ple), "w_ple_gate": w_ple_gate[i].astype(BF16),
            "w_ple_proj": w_ple_proj[i].astype(BF16), "g_final": g_final.reshape(1, -1).astype(F32),
        }
        xp = _encoder_group(xp, p_prompt[i], w, i == depth - 1)
        xs = _encoder_group(xs, p_sample[i], w, i == depth - 1)
    return (xp, xs)
```

```python
import functools

import jax
import jax.numpy as jnp
import numpy as np
from jax import lax
from jax.experimental import pallas as pl
from jax.experimental.pallas import tpu as pltpu
from jax.experimental.pallas import tpu_sc as plsc

D_MODEL = 1024
GRID_W = 64
CONV_CH = 512
CONV_K = 31
NA_HEADS = 16
NA_HEAD_DIM = 32
NA_WIDTH = NA_HEADS * NA_HEAD_DIM
WIN_R = 8
WIN_C = 16
N_EXPERTS = 64
TOP_K = 8
N_GROUP = 8
TOPK_GROUP = 4
EXPERT_FF = 256
SHARED_FF = 256
ROUTED_SCALE = 2.5
PLE_DIM = 256
EPS = 1e-6
NEG_BIG = -1e30

OFF_Q = 2 * CONV_CH
OFF_K = OFF_Q + NA_WIDTH
OFF_V = OFF_K + NA_WIDTH
OFF_GATE = OFF_V + NA_WIDTH

CONV_HALO = 16
VMEM_LIMIT = 56 * 1024 * 1024
LANES = 128
HALF_D = D_MODEL // 2
HI_HALF_MASK = -65536
EXPERT_BLOCK = 512
SC_CHUNK = 64

BF16 = jnp.bfloat16
F32 = jnp.float32


def _cparams(sem):
    return pltpu.CompilerParams(dimension_semantics=sem, vmem_limit_bytes=VMEM_LIMIT)


def _const_spec(shape):
    nd = len(shape)
    return pl.BlockSpec(shape, lambda *_: (0,) * nd)


def _rms(xf, g):
    return xf * lax.rsqrt(jnp.mean(xf * xf, axis=-1, keepdims=True) + EPS) * g


def _sigmoid(x):
    return 1.0 / (1.0 + jnp.exp(-x))


def _in_proj_kernel(x_ref, g_ref, w_ref, b_ref, glu_ref, q_ref, k_ref, v_ref, sgc_ref, sgn_ref):
    h = _rms(x_ref[...], g_ref[...]).astype(BF16)

    def proj(lo, hi):
        return jnp.dot(h, w_ref[:, lo:hi], preferred_element_type=F32) + b_ref[:, lo:hi]

    u = proj(0, CONV_CH)
    gate = proj(CONV_CH, 2 * CONV_CH)
    glu_ref[...] = (u * _sigmoid(gate)).astype(BF16)
    q_ref[...] = (proj(OFF_Q, OFF_K) * (NA_HEAD_DIM ** -0.5)).astype(BF16)
    k_ref[...] = proj(OFF_K, OFF_V).astype(BF16)
    v_ref[...] = proj(OFF_V, OFF_GATE).astype(BF16)
    for c in range(2):
        lo = OFF_GATE + c * CONV_CH
        sgc_ref[:, c * CONV_CH:(c + 1) * CONV_CH] = _sigmoid(proj(lo, lo + CONV_CH)).astype(BF16)
        lo = OFF_GATE + D_MODEL + c * CONV_CH
        sgn_ref[:, c * CONV_CH:(c + 1) * CONV_CH] = _sigmoid(proj(lo, lo + CONV_CH)).astype(BF16)


def _in_proj(x, g_mix, w_in, b_in, tm):
    n = x.shape[0]
    cols = w_in.shape[1]
    row = lambda w: pl.BlockSpec((tm, w), lambda i: (i, 0))
    out_w = (CONV_CH, NA_WIDTH, NA_WIDTH, NA_WIDTH, D_MODEL, D_MODEL)
    return pl.pallas_call(
        _in_proj_kernel,
        grid=(n // tm,),
        in_specs=[row(D_MODEL), _const_spec((1, D_MODEL)), _const_spec((D_MODEL, cols)),
                  _const_spec((1, cols))],
        out_specs=[row(w) for w in out_w],
        out_shape=[jax.ShapeDtypeStruct((n, w), BF16) for w in out_w],
        compiler_params=_cparams(("parallel",)),
        name="in_proj",
    )(x, g_mix, w_in, b_in)


def _conv_kernel(prev_ref, cur_ref, next_ref, w_ref, b_ref, lg_ref, lb_ref, o_ref, ext_ref, *, ts):
    i = pl.program_id(1)
    last = pl.num_programs(1) - 1
    prev = prev_ref[0].astype(F32)
    nxt = next_ref[0].astype(F32)
    ext_ref[0:CONV_HALO, :] = jnp.where(i > 0, prev, 0.0)
    ext_ref[CONV_HALO:CONV_HALO + ts, :] = cur_ref[0].astype(F32)
    ext_ref[CONV_HALO + ts:, :] = jnp.where(i < last, nxt, 0.0)
    base = CONV_HALO - CONV_K // 2
    acc = jnp.zeros((ts, CONV_CH), F32)
    for j in range(CONV_K):
        acc = acc + ext_ref[base + j:base + j + ts, :] * w_ref[j:j + 1, :]
    y = acc + b_ref[...]
    mu = jnp.mean(y, axis=-1, keepdims=True)
    yc = y - mu
    var = jnp.mean(yc * yc, axis=-1, keepdims=True)
    z = yc * lax.rsqrt(var + EPS) * lg_ref[...] + lb_ref[...]
    o_ref[0] = (z * _sigmoid(z)).astype(BF16)


def _conv_branch(glu, conv_w, conv_b, ln_g, ln_b, ts):
    bsz, seq, ch = glu.shape
    nh = ts // CONV_HALO
    n_halo = seq // CONV_HALO
    return pl.pallas_call(
        functools.partial(_conv_kernel, ts=ts),
        grid=(bsz, seq // ts),
        in_specs=[
            pl.BlockSpec((1, CONV_HALO, ch), lambda b, i: (b, jnp.maximum(i * nh - 1, 0), 0)),
            pl.BlockSpec((1, ts, ch), lambda b, i: (b, i, 0)),
            pl.BlockSpec((1, CONV_HALO, ch), lambda b, i: (b, jnp.minimum((i + 1) * nh, n_halo - 1), 0)),
            _const_spec((CONV_K, ch)), _const_spec((1, ch)), _const_spec((1, ch)), _const_spec((1, ch)),
        ],
        out_specs=pl.BlockSpec((1, ts, ch), lambda b, i: (b, i, 0)),
        out_shape=jax.ShapeDtypeStruct((bsz, seq, ch), BF16),
        scratch_shapes=[pltpu.VMEM((ts + 2 * CONV_HALO, ch), F32)],
        compiler_params=_cparams(("parallel", "parallel")),
        name="conv_branch",
    )(glu, glu, glu, conv_w, conv_b, ln_g, ln_b)


def _na_bias_table(rpb):
    col = np.arange(GRID_W)
    col_start = np.clip(col - WIN_C // 2, 0, GRID_W - WIN_C)
    kc = np.arange(GRID_W)
    inside = (kc[None, :] >= col_start[:, None]) & (kc[None, :] < col_start[:, None] + WIN_C)
    dc = kc[None, :] - col[:, None] + (WIN_C - 1)
    n_dc = 2 * WIN_C - 1
    onehot = (inside[:, :, None] & (dc[:, :, None] == np.arange(n_dc))).astype(np.float32)
    t = jnp.einsum("hrd,ckd->hrck", rpb.astype(F32), jnp.asarray(onehot),
                   precision=lax.Precision.HIGHEST)
    t = jnp.where(jnp.asarray(inside)[None, None], t, NEG_BIG)
    t = jnp.stack([t[:, o:o + WIN_R] for o in range(WIN_R)], axis=0)
    t = jnp.transpose(t, (0, 1, 3, 2, 4))
    return t.reshape(WIN_R, NA_HEADS, GRID_W, WIN_R * GRID_W)


def _na_kernel(q_ref, kp_ref, kc_ref, kn_ref, vp_ref, vc_ref, vn_ref, tab_ref, o_ref,
               kband_ref, vband_ref, *, rows, rb_rows):
    blk = rb_rows * GRID_W
    rb = pl.program_id(1)
    kband_ref[0:blk, :] = kp_ref[0]
    kband_ref[blk:2 * blk, :] = kc_ref[0]
    kband_ref[2 * blk:, :] = kn_ref[0]
    vband_ref[0:blk, :] = vp_ref[0]
    vband_ref[blk:2 * blk, :] = vc_ref[0]
    vband_ref[2 * blk:, :] = vn_ref[0]
    band = WIN_R * GRID_W

    def row_body(i, carry):
        r = rb * rb_rows + i
        r0 = jnp.clip(r - WIN_R // 2, 0, rows - WIN_R)
        start = pl.multiple_of((r0 - (rb - 1) * rb_rows) * GRID_W, GRID_W)
        o = r0 - r + (WIN_R - 1)
        kb = kband_ref[pl.ds(start, band), :]
        vb = vband_ref[pl.ds(start, band), :]
        qoff = pl.multiple_of(i * GRID_W, GRID_W)
        qi = q_ref[0, pl.ds(qoff, GRID_W), :]
        outs = []
        for h in range(NA_HEADS):
            sl = slice(h * NA_HEAD_DIM, (h + 1) * NA_HEAD_DIM)
            s = lax.dot_general(qi[:, sl], kb[:, sl], (((1,), (1,)), ((), ())),
                                preferred_element_type=F32)
            s = s + tab_ref[o, h]
            m = jnp.max(s, axis=-1, keepdims=True)
            p = jnp.exp(s - m)
            l = jnp.sum(p, axis=-1, keepdims=True)
            oh = jnp.dot(p.astype(BF16), vb[:, sl], preferred_element_type=F32)
            outs.append(oh / l)
        o_ref[0, pl.ds(qoff, GRID_W), :] = jnp.concatenate(outs, axis=-1).astype(BF16)
        return carry

    lax.fori_loop(0, rb_rows, row_body, 0)


def _natten(q, k, v, tab, rb_rows):
    bsz, seq, width = q.shape
    rows = seq // GRID_W
    nrb = rows // rb_rows
    blk = rb_rows * GRID_W
    cur = pl.BlockSpec((1, blk, width), lambda b, i: (b, i, 0))
    prv = pl.BlockSpec((1, blk, width), lambda b, i: (b, jnp.maximum(i - 1, 0), 0))
    nxt = pl.BlockSpec((1, blk, width), lambda b, i: (b, jnp.minimum(i + 1, nrb - 1), 0))
    return pl.pallas_call(
        functools.partial(_na_kernel, rows=rows, rb_rows=rb_rows),
        grid=(bsz, nrb),
        in_specs=[cur, prv, cur, nxt, prv, cur, nxt, _const_spec(tab.shape)],
        out_specs=cur,
        out_shape=jax.ShapeDtypeStruct((bsz, seq, width), BF16),
        scratch_shapes=[pltpu.VMEM((3 * blk, width), BF16), pltpu.VMEM((3 * blk, width), BF16)],
        compiler_params=_cparams(("parallel", "parallel")),
        name="natten",
    )(q, k, k, k, v, v, v, tab)


def _pack_bf16_pair(lo, hi):
    lo_bits = lax.bitcast_convert_type(lo.astype(BF16).astype(F32), jnp.int32)
    hi_bits = lax.bitcast_convert_type(hi.astype(BF16).astype(F32), jnp.int32)
    return lax.shift_right_logical(lo_bits, 16) | (hi_bits & HI_HALF_MASK)


def _unpack_bf16_pair(word):
    lo = lax.bitcast_convert_type(lax.shift_left(word, 16), F32)
    hi = lax.bitcast_convert_type(word & HI_HALF_MASK, F32)
    return lo, hi


def _route(logits_t, bias_col):
    e, t = logits_t.shape
    per_group = e // N_GROUP
    scores = _sigmoid(logits_t)
    biased = scores + bias_col
    b3 = biased.reshape(N_GROUP, per_group, t)
    idx3 = lax.broadcasted_iota(jnp.int32, b3.shape, 1)
    m1 = jnp.max(b3, axis=1, keepdims=True)
    first = jnp.min(jnp.where(b3 == m1, idx3, per_group), axis=1, keepdims=True)
    m2 = jnp.max(jnp.where(idx3 == first, -jnp.inf, b3), axis=1, keepdims=True)
    gscore = (m1 + m2).reshape(N_GROUP, t)
    gidx = lax.broadcasted_iota(jnp.int32, gscore.shape, 0)
    grank = jnp.zeros(gscore.shape, jnp.int32)
    for g in range(N_GROUP):
        other = gscore[g:g + 1, :]
        beats = (other > gscore) | ((other == gscore) & (g < gidx))
        grank = grank + beats.astype(jnp.int32)
    gsel = (grank < TOPK_GROUP).reshape(N_GROUP, 1, t)
    masked = jnp.where(gsel, b3, -jnp.inf).reshape(e, t)
    eidx = lax.broadcasted_iota(jnp.int32, masked.shape, 0)
    rank = jnp.zeros(masked.shape, jnp.int32)
    for j in range(e):
        other = masked[j:j + 1, :]
        beats = (other > masked) | ((other == masked) & (j < eidx))
        rank = rank + beats.astype(jnp.int32)
    sel = rank < TOP_K
    w = jnp.where(sel, scores, 0.0)
    return sel, w / jnp.sum(w, axis=0, keepdims=True) * ROUTED_SCALE


def _mix_kernel(x_ref, ca_ref, na_ref, sgc_ref, sgn_ref, wco_ref, wao_ref, wout_ref, g_ref,
                wr_ref, rb_ref, tri_ref, x1_ref, h2p_ref, te_ref, tw_ref, pos_ref, cnt_ref, base_ref):
    @pl.when(pl.program_id(0) == 0)
    def _():
        base_ref[...] = jnp.zeros_like(base_ref)

    yc = jnp.dot(ca_ref[...], wco_ref[...], preferred_element_type=F32)
    yn = jnp.dot(na_ref[...], wao_ref[...], preferred_element_type=F32)
    m = sgc_ref[...].astype(F32) * yc + sgn_ref[...].astype(F32) * yn
    x1 = x_ref[...] + jnp.dot(m.astype(BF16), wout_ref[...], preferred_element_type=F32)
    x1_ref[...] = x1
    h2 = _rms(x1, g_ref[...])
    h2p_ref[...] = _pack_bf16_pair(h2[:, :HALF_D], h2[:, HALF_D:])
    logits_t = lax.dot_general(wr_ref[...], h2, (((1,), (1,)), ((), ())),
                               preferred_element_type=F32, precision=lax.Precision.HIGHEST)
    sel, w = _route(logits_t, rb_ref[...])
    sel_b = jnp.where(sel, 1.0, 0.0).astype(BF16)
    pos = base_ref[:, 0:1] + jnp.dot(sel_b, tri_ref[...], preferred_element_type=F32)
    base_ref[...] = base_ref[...] + jnp.sum(sel_b.astype(F32), axis=1, keepdims=True)
    cnt_ref[...] = base_ref[...].astype(jnp.int32)
    ne = sel_b.shape[0]
    lower = jnp.where(lax.broadcasted_iota(jnp.int32, (ne, ne), 0) > lax.broadcasted_iota(jnp.int32, (ne, ne), 1),
                      1.0, 0.0).astype(BF16)
    slot = jnp.dot(lower, sel_b, preferred_element_type=F32)
    eidx = lax.broadcasted_iota(jnp.int32, sel.shape, 0).astype(F32)
    rows_e, rows_w, rows_p = [], [], []
    for k in range(TOP_K):
        mk = sel & (slot == float(k))
        pick = lambda a: jnp.sum(jnp.where(mk, a, 0.0), axis=0, keepdims=True)
        rows_e.append(pick(eidx))
        rows_w.append(pick(w))
        rows_p.append(pick(pos))
    te_ref[...] = jnp.concatenate(rows_e, axis=0).astype(jnp.int32)
    tw_ref[...] = jnp.concatenate(rows_w, axis=0)
    pos_ref[...] = jnp.concatenate(rows_p, axis=0).astype(jnp.int32)


def _mix_route(x, ca, na, sgc, sgn, wco, wao, wout, g_ffn, w_router_t, router_bias, tm):
    n = x.shape[0]
    row = lambda w: pl.BlockSpec((tm, w), lambda i: (i, 0))
    slots = pl.BlockSpec((TOP_K, tm), lambda i: (0, i))
    tri = (np.arange(tm)[:, None] < np.arange(tm)[None, :]).astype(np.float32)
    return pl.pallas_call(
        _mix_kernel,
        grid=(n // tm,),
        in_specs=[row(D_MODEL), row(CONV_CH), row(NA_WIDTH), row(D_MODEL), row(D_MODEL),
                  _const_spec(wco.shape), _const_spec(wao.shape), _const_spec(wout.shape),
                  _const_spec((1, D_MODEL)), _const_spec(w_router_t.shape), _const_spec((N_EXPERTS, 1)),
                  _const_spec((tm, tm))],
        out_specs=[row(D_MODEL), row(HALF_D), slots, slots, slots, _const_spec((N_EXPERTS, LANES))],
        out_shape=[jax.ShapeDtypeStruct((n, D_MODEL), F32), jax.ShapeDtypeStruct((n, HALF_D), jnp.int32),
                   jax.ShapeDtypeStruct((TOP_K, n), jnp.int32), jax.ShapeDtypeStruct((TOP_K, n), F32),
                   jax.ShapeDtypeStruct((TOP_K, n), jnp.int32),
                   jax.ShapeDtypeStruct((N_EXPERTS, LANES), jnp.int32)],
        scratch_shapes=[pltpu.VMEM((N_EXPERTS, LANES), F32)],
        compiler_params=_cparams(("arbitrary",)),
        name="mix_route",
    )(x, ca, na, sgc, sgn, wco, wao, wout, g_ffn, w_router_t, router_bias, jnp.asarray(tri, BF16))


def _dispatch_plan(top_e, pos, cnt, n_blocks):
    counts = cnt[:, 0]
    padded = ((counts + EXPERT_BLOCK - 1) // EXPERT_BLOCK) * EXPERT_BLOCK
    ends = jnp.cumsum(padded)
    starts = ends - padded
    dest = jnp.take(starts, top_e, axis=0) + pos
    block_first_row = jnp.arange(n_blocks, dtype=jnp.int32) * EXPERT_BLOCK
    block_expert = jnp.minimum(jnp.searchsorted(ends, block_first_row, side="right"),
                               N_EXPERTS - 1).astype(jnp.int32)
    n_used = (ends[-1:] // EXPERT_BLOCK).astype(jnp.int32)
    return dest.astype(jnp.int32), block_expert, n_used


def _sc_mesh():
    return plsc.VectorSubcoreMesh(core_axis_name="core", subcore_axis_name="subcore")


def _sc_worker_id(info):
    return lax.axis_index("subcore") * info.num_cores + lax.axis_index("core")


def _sc_dispatch(h2p, dest, n_rows):
    n, width = h2p.shape
    info = plsc.get_sparse_core_info()
    n_chunks = n // (info.num_cores * info.num_subcores * SC_CHUNK)
    dest3 = jnp.transpose(dest.reshape(TOP_K, n // SC_CHUNK, SC_CHUNK), (1, 0, 2))

    @functools.partial(
        pl.kernel, mesh=_sc_mesh(),
        out_type=jax.ShapeDtypeStruct((n_rows, width), h2p.dtype),
        scratch_types=[pltpu.VMEM((TOP_K, SC_CHUNK), jnp.int32), pltpu.VMEM((SC_CHUNK, width), h2p.dtype),
                       pltpu.SemaphoreType.DMA],
        name="moe_dispatch",
    )
    def dispatch(h_hbm, dest_hbm, xs_hbm, idx_v, rows_v, sem):
        wid = _sc_worker_id(info)

        @pl.loop(0, n_chunks)
        def _(i):
            c = wid * n_chunks + i
            pltpu.sync_copy(dest_hbm.at[c], idx_v)
            pltpu.sync_copy(h_hbm.at[pl.ds(c * SC_CHUNK, SC_CHUNK)], rows_v)
            copies = [pltpu.async_copy(rows_v, xs_hbm.at[idx_v.at[k]], sem) for k in range(TOP_K)]
            for cp in copies:
                cp.wait()

    return dispatch(h2p, dest3)


def _sc_combine(ys, dest):
    width = ys.shape[1]
    n_idx = dest.size
    info = plsc.get_sparse_core_info()
    per_worker = n_idx // (info.num_cores * info.num_subcores)
    n_chunks = per_worker // SC_CHUNK

    @functools.partial(
        pl.kernel, mesh=_sc_mesh(),
        out_type=jax.ShapeDtypeStruct((n_idx, width), ys.dtype),
        scratch_types=[pltpu.VMEM((SC_CHUNK,), jnp.int32), pltpu.VMEM((SC_CHUNK, width), ys.dtype),
                       pltpu.SemaphoreType.DMA],
        name="moe_combine",
    )
    def combine(ys_hbm, idx_hbm, yg_hbm, idx_v, rows_v, sem):
        base = _sc_worker_id(info) * per_worker

        @pl.loop(0, n_chunks)
        def _(i):
            off = base + i * SC_CHUNK
            pltpu.sync_copy(idx_hbm.at[pl.ds(off, SC_CHUNK)], idx_v)
            pltpu.async_copy(ys_hbm.at[idx_v], rows_v, sem).wait()
            pltpu.sync_copy(rows_v, yg_hbm.at[pl.ds(off, SC_CHUNK)])

    return combine(ys, dest.reshape(-1)).reshape(dest.shape + (width,))


def _experts_kernel(be_ref, nu_ref, xs_ref, wgu_ref, wd_ref, ys_ref):
    b = pl.program_id(0)

    @pl.when(b < nu_ref[0])
    def _():
        lo, hi = _unpack_bf16_pair(xs_ref[...])
        gu = (jnp.dot(lo.astype(BF16), wgu_ref[0, :HALF_D, :], preferred_element_type=F32)
              + jnp.dot(hi.astype(BF16), wgu_ref[0, HALF_D:, :], preferred_element_type=F32))
        g = gu[:, :EXPERT_FF]
        act = g * _sigmoid(g) * gu[:, EXPERT_FF:]
        y = jnp.dot(act.astype(BF16), wd_ref[0], preferred_element_type=F32)
        ys_ref[...] = _pack_bf16_pair(y[:, :HALF_D], y[:, HALF_D:])

    @pl.when(b >= nu_ref[0])
    def _():
        ys_ref[...] = jnp.zeros_like(ys_ref)


def _experts(xs, block_expert, n_used, wgu, wd):
    n_rows, width = xs.shape
    rows = pl.BlockSpec((EXPERT_BLOCK, width), lambda b, be, nu: (b, 0))
    return pl.pallas_call(
        _experts_kernel,
        grid_spec=pltpu.PrefetchScalarGridSpec(
            num_scalar_prefetch=2,
            grid=(n_rows // EXPERT_BLOCK,),
            in_specs=[rows,
                      pl.BlockSpec((1,) + wgu.shape[1:], lambda b, be, nu: (be[b], 0, 0)),
                      pl.BlockSpec((1,) + wd.shape[1:], lambda b, be, nu: (be[b], 0, 0))],
            out_specs=rows,
        ),
        out_shape=jax.ShapeDtypeStruct((n_rows, width), jnp.int32),
        compiler_params=_cparams(("parallel",)),
        name="moe_experts",
    )(block_expert, n_used, xs, wgu, wd)


def _tail_kernel(x1_ref, yg_ref, tw_ref, h2p_ref, p_ref, wsg_ref, wsd_ref, gp_ref, wpg_ref, wpp_ref, gf_ref,
                 o_ref, *, final_norm):
    ylo = yhi = None
    for k in range(TOP_K):
        lo, hi = _unpack_bf16_pair(yg_ref[k])
        wk = tw_ref[:, k:k + 1]
        ylo = wk * lo if ylo is None else ylo + wk * lo
        yhi = wk * hi if yhi is None else yhi + wk * hi
    hlo, hhi = _unpack_bf16_pair(h2p_ref[...])
    gu = (jnp.dot(hlo.astype(BF16), wsg_ref[:HALF_D, :], preferred_element_type=F32)
          + jnp.dot(hhi.astype(BF16), wsg_ref[HALF_D:, :], preferred_element_type=F32))
    g = gu[:, :SHARED_FF]
    act = g * _sigmoid(g) * gu[:, SHARED_FF:]
    sh = jnp.dot(act.astype(BF16), wsd_ref[...], preferred_element_type=F32)
    x2 = x1_ref[...] + jnp.concatenate([ylo, yhi], axis=-1) + sh
    h3 = _rms(x2, gp_ref[...]).astype(BF16)
    gate = _sigmoid(jnp.dot(h3, wpg_ref[...], preferred_element_type=F32))
    pp = jnp.dot(p_ref[...].astype(BF16), wpp_ref[...], preferred_element_type=F32)
    x3 = x2 + pp * gate
    o_ref[...] = _rms(x3, gf_ref[...]) if final_norm else x3


def _tail(x1, yg, tw, h2p, p, wsg, wsd, g_ple, wpg, wpp, g_final, tm, final_norm):
    n = x1.shape[0]
    row = lambda w: pl.BlockSpec((tm, w), lambda i: (i, 0))
    return pl.pallas_call(
        functools.partial(_tail_kernel, final_norm=final_norm),
        grid=(n // tm,),
        in_specs=[row(D_MODEL), pl.BlockSpec((TOP_K, tm, HALF_D), lambda i: (0, i, 0)), row(TOP_K),
                  row(HALF_D), row(PLE_DIM),
                  _const_spec(wsg.shape), _const_spec(wsd.shape), _const_spec((1, D_MODEL)),
                  _const_spec(wpg.shape), _const_spec(wpp.shape), _const_spec((1, D_MODEL))],
        out_specs=row(D_MODEL),
        out_shape=jax.ShapeDtypeStruct((n, D_MODEL), F32),
        compiler_params=_cparams(("parallel",)),
        name="tail",
    )(x1, yg, tw, h2p, p, wsg, wsd, g_ple, wpg, wpp, g_final)


def _tile(n, pref):
    t = min(pref, n)
    while n % t:
        t //= 2
    return t


def _encoder_group(x, p, w, final_norm):
    bsz, seq, d = x.shape
    n = bsz * seq
    xf = x.reshape(n, d)
    tm = _tile(n, 512)
    glu, q, k, v, sgc, sgn = _in_proj(xf, w["g_mix"], w["w_in"], w["b_in"], tm)
    ca = _conv_branch(glu.reshape(bsz, seq, CONV_CH), w["conv_w"], w["conv_b"], w["conv_ln_g"],
                      w["conv_ln_b"], _tile(seq, 512))
    b3 = lambda a: a.reshape(bsz, seq, NA_WIDTH)
    na = _natten(b3(q), b3(k), b3(v), w["na_tab"], _tile(seq // GRID_W, 8))
    x1, h2p, top_e, top_w, pos, cnt = _mix_route(
        xf, ca.reshape(n, CONV_CH), na.reshape(n, NA_WIDTH), sgc, sgn, w["w_conv_out"], w["w_attn_out"],
        w["w_out"], w["g_ffn"], w["w_router_t"], w["router_bias"], tm)
    n_blocks = n * TOP_K // EXPERT_BLOCK + N_EXPERTS
    dest, block_expert, n_used = _dispatch_plan(top_e, pos, cnt, n_blocks)
    xs = _sc_dispatch(h2p, dest, n_blocks * EXPERT_BLOCK)
    ys = _experts(xs, block_expert, n_used, w["w_exp_gu"], w["w_exp_down"])
    yg = _sc_combine(ys, dest)
    out = _tail(x1, yg, top_w.T, h2p, p.reshape(n, PLE_DIM), w["w_sh_gu"], w["w_sh_down"], w["g_ple"],
                w["w_ple_gate"], w["w_ple_proj"], w["g_final"], _tile(n, 256), final_norm)
    return out.reshape(bsz, seq, d)


def kernel(x_prompt, x_sample, p_prompt, p_sample, g_mix, w_in, b_in, conv_w, conv_b, conv_ln_g,
           conv_ln_b, w_conv_out, rpb, w_attn_out, w_out, g_ffn, w_router, router_bias, w_exp_gu,
           w_exp_down, w_sh_gu, w_sh_down, g_ple, w_ple_gate, w_ple_proj, g_final):
    depth = g_mix.shape[0]
    xp, xs = x_prompt, x_sample
    for i in range(depth):
        row = lambda a: a[i].reshape(1, -1).astype(F32)
        w = {
            "g_mix": row(g_mix), "w_in": w_in[i].astype(BF16), "b_in": row(b_in),
            "conv_w": conv_w[i].astype(F32), "conv_b": row(conv_b),
            "conv_ln_g": row(conv_ln_g), "conv_ln_b": row(conv_ln_b),
            "w_conv_out": w_conv_out[i].astype(BF16), "na_tab": _na_bias_table(rpb[i]),
            "w_attn_out": w_attn_out[i].astype(BF16), "w_out": w_out[i].astype(BF16),
            "g_ffn": row(g_ffn), "w_router_t": w_router[i].T.astype(F32),
            "router_bias": router_bias[i].reshape(-1, 1).astype(F32),
            "w_exp_gu": w_exp_gu[i].astype(BF16), "w_exp_down": w_exp_down[i].astype(BF16),
            "w_sh_gu": w_sh_gu[i].astype(BF16), "w_sh_down": w_sh_down[i].astype(BF16),
            "g_ple": row(g_ple), "w_ple_gate": w_ple_gate[i].astype(BF16),
            "w_ple_proj": w_ple_proj[i].astype(BF16), "g_final": g_final.reshape(1, -1).astype(F32),
        }
        xp = _encoder_group(xp, p_prompt[i], w, i == depth - 1)
        xs = _encoder_group(xs, p_sample[i], w, i == depth - 1)
    return (xp, xs)
```

```python
import functools

import jax
import jax.numpy as jnp
import numpy as np
from jax import lax
from jax.experimental import pallas as pl
from jax.experimental.pallas import tpu as pltpu
from jax.experimental.pallas import tpu_sc as plsc

D_MODEL = 1024
GRID_W = 64
CONV_CH = 512
CONV_K = 31
NA_HEADS = 16
NA_HEAD_DIM = 32
NA_WIDTH = NA_HEADS * NA_HEAD_DIM
WIN_R = 8
WIN_C = 16
N_EXPERTS = 64
TOP_K = 8
N_GROUP = 8
TOPK_GROUP = 4
EXPERT_FF = 256
SHARED_FF = 256
ROUTED_SCALE = 2.5
PLE_DIM = 256
EPS = 1e-6
NEG_BIG = -1e30

OFF_Q = 2 * CONV_CH
OFF_K = OFF_Q + NA_WIDTH
OFF_V = OFF_K + NA_WIDTH
OFF_GATE = OFF_V + NA_WIDTH

CONV_HALO = 16
VMEM_LIMIT = 56 * 1024 * 1024
LANES = 128
HEADS_PER_GROUP = LANES // NA_HEAD_DIM
NA_GROUPS = NA_HEADS // HEADS_PER_GROUP
HALF_D = D_MODEL // 2
HI_HALF_MASK = -65536
EXPERT_BLOCK = 512
SC_CHUNK = 64

BF16 = jnp.bfloat16
F32 = jnp.float32


def _cparams(sem):
    return pltpu.CompilerParams(dimension_semantics=sem, vmem_limit_bytes=VMEM_LIMIT)


def _const_spec(shape):
    nd = len(shape)
    return pl.BlockSpec(shape, lambda *_: (0,) * nd)


def _rms(xf, g):
    return xf * lax.rsqrt(jnp.mean(xf * xf, axis=-1, keepdims=True) + EPS) * g


def _sigmoid(x):
    return 1.0 / (1.0 + jnp.exp(-x))


def _in_proj_kernel(x_ref, g_ref, w_ref, b_ref, glu_ref, q_ref, k_ref, v_ref, sgc_ref, sgn_ref):
    h = _rms(x_ref[...], g_ref[...]).astype(BF16)

    def proj(lo, hi):
        return jnp.dot(h, w_ref[:, lo:hi], preferred_element_type=F32) + b_ref[:, lo:hi]

    u = proj(0, CONV_CH)
    gate = proj(CONV_CH, 2 * CONV_CH)
    glu_ref[...] = (u * _sigmoid(gate)).astype(BF16)
    q_ref[...] = (proj(OFF_Q, OFF_K) * (NA_HEAD_DIM ** -0.5)).astype(BF16)
    k_ref[...] = proj(OFF_K, OFF_V).astype(BF16)
    v_ref[...] = proj(OFF_V, OFF_GATE).astype(BF16)
    for c in range(2):
        lo = OFF_GATE + c * CONV_CH
        sgc_ref[:, c * CONV_CH:(c + 1) * CONV_CH] = _sigmoid(proj(lo, lo + CONV_CH)).astype(BF16)
        lo = OFF_GATE + D_MODEL + c * CONV_CH
        sgn_ref[:, c * CONV_CH:(c + 1) * CONV_CH] = _sigmoid(proj(lo, lo + CONV_CH)).astype(BF16)


def _in_proj(x, g_mix, w_in, b_in, tm):
    n = x.shape[0]
    cols = w_in.shape[1]
    row = lambda w: pl.BlockSpec((tm, w), lambda i: (i, 0))
    out_w = (CONV_CH, NA_WIDTH, NA_WIDTH, NA_WIDTH, D_MODEL, D_MODEL)
    return pl.pallas_call(
        _in_proj_kernel,
        grid=(n // tm,),
        in_specs=[row(D_MODEL), _const_spec((1, D_MODEL)), _const_spec((D_MODEL, cols)),
                  _const_spec((1, cols))],
        out_specs=[row(w) for w in out_w],
        out_shape=[jax.ShapeDtypeStruct((n, w), BF16) for w in out_w],
        compiler_params=_cparams(("parallel",)),
        name="in_proj",
    )(x, g_mix, w_in, b_in)


def _conv_kernel(prev_ref, cur_ref, next_ref, w_ref, b_ref, lg_ref, lb_ref, o_ref, ext_ref, *, ts):
    i = pl.program_id(1)
    last = pl.num_programs(1) - 1
    prev = prev_ref[0].astype(F32)
    nxt = next_ref[0].astype(F32)
    ext_ref[0:CONV_HALO, :] = jnp.where(i > 0, prev, 0.0)
    ext_ref[CONV_HALO:CONV_HALO + ts, :] = cur_ref[0].astype(F32)
    ext_ref[CONV_HALO + ts:, :] = jnp.where(i < last, nxt, 0.0)
    base = CONV_HALO - CONV_K // 2
    acc = jnp.zeros((ts, CONV_CH), F32)
    for j in range(CONV_K):
        acc = acc + ext_ref[base + j:base + j + ts, :] * w_ref[j:j + 1, :]
    y = acc + b_ref[...]
    mu = jnp.mean(y, axis=-1, keepdims=True)
    yc = y - mu
    var = jnp.mean(yc * yc, axis=-1, keepdims=True)
    z = yc * lax.rsqrt(var + EPS) * lg_ref[...] + lb_ref[...]
    o_ref[0] = (z * _sigmoid(z)).astype(BF16)


def _conv_branch(glu, conv_w, conv_b, ln_g, ln_b, ts):
    bsz, seq, ch = glu.shape
    nh = ts // CONV_HALO
    n_halo = seq // CONV_HALO
    return pl.pallas_call(
        functools.partial(_conv_kernel, ts=ts),
        grid=(bsz, seq // ts),
        in_specs=[
            pl.BlockSpec((1, CONV_HALO, ch), lambda b, i: (b, jnp.maximum(i * nh - 1, 0), 0)),
            pl.BlockSpec((1, ts, ch), lambda b, i: (b, i, 0)),
            pl.BlockSpec((1, CONV_HALO, ch), lambda b, i: (b, jnp.minimum((i + 1) * nh, n_halo - 1), 0)),
            _const_spec((CONV_K, ch)), _const_spec((1, ch)), _const_spec((1, ch)), _const_spec((1, ch)),
        ],
        out_specs=pl.BlockSpec((1, ts, ch), lambda b, i: (b, i, 0)),
        out_shape=jax.ShapeDtypeStruct((bsz, seq, ch), BF16),
        scratch_shapes=[pltpu.VMEM((ts + 2 * CONV_HALO, ch), F32)],
        compiler_params=_cparams(("parallel", "parallel")),
        name="conv_branch",
    )(glu, glu, glu, conv_w, conv_b, ln_g, ln_b)


def _na_bias_table(rpb):
    col = np.arange(GRID_W)
    col_start = np.clip(col - WIN_C // 2, 0, GRID_W - WIN_C)
    kc = np.arange(GRID_W)
    inside = (kc[None, :] >= col_start[:, None]) & (kc[None, :] < col_start[:, None] + WIN_C)
    dc = kc[None, :] - col[:, None] + (WIN_C - 1)
    n_dc = 2 * WIN_C - 1
    onehot = (inside[:, :, None] & (dc[:, :, None] == np.arange(n_dc))).astype(np.float32)
    t = jnp.einsum("hrd,ckd->hrck", rpb.astype(F32), jnp.asarray(onehot),
                   precision=lax.Precision.HIGHEST)
    t = jnp.where(jnp.asarray(inside)[None, None], t, NEG_BIG)
    t = jnp.stack([t[:, o:o + WIN_R] for o in range(WIN_R)], axis=0)
    t = jnp.transpose(t, (0, 1, 3, 2, 4))
    return t.reshape(WIN_R, NA_GROUPS, HEADS_PER_GROUP * GRID_W, WIN_R * GRID_W)


def _na_kernel(q_ref, kp_ref, kc_ref, kn_ref, vp_ref, vc_ref, vn_ref, tab_ref, hmask_ref, o_ref,
               kband_ref, vaug_ref, s_ref, p_ref, *, rows, rb_rows):
    blk = rb_rows * GRID_W
    rb = pl.program_id(1)
    kband_ref[0:blk, :] = kp_ref[0]
    kband_ref[blk:2 * blk, :] = kc_ref[0]
    kband_ref[2 * blk:, :] = kn_ref[0]
    ones = jnp.ones((blk, LANES), BF16)
    for part, v_ref in enumerate((vp_ref, vc_ref, vn_ref)):
        rs = slice(part * blk, (part + 1) * blk)
        for g in range(NA_GROUPS):
            vaug_ref[rs, 2 * g * LANES:(2 * g + 1) * LANES] = v_ref[0, :, g * LANES:(g + 1) * LANES]
            vaug_ref[rs, (2 * g + 1) * LANES:(2 * g + 2) * LANES] = ones
    band = WIN_R * GRID_W
    lane_head = lax.broadcasted_iota(jnp.int32, (GRID_W, LANES), 1) // NA_HEAD_DIM

    def row_body(i, carry):
        r = rb * rb_rows + i
        r0 = jnp.clip(r - WIN_R // 2, 0, rows - WIN_R)
        start = pl.multiple_of((r0 - (rb - 1) * rb_rows) * GRID_W, GRID_W)
        o = r0 - r + (WIN_R - 1)
        qoff = pl.multiple_of(i * GRID_W, GRID_W)
        qi = q_ref[0, pl.ds(qoff, GRID_W), :]
        for g in range(NA_GROUPS):
            gl = slice(g * LANES, (g + 1) * LANES)
            q4 = jnp.concatenate([qi[:, gl]] * HEADS_PER_GROUP, axis=0) * hmask_ref[...]
            s = lax.dot_general(q4, kband_ref[pl.ds(start, band), gl], (((1,), (1,)), ((), ())),
                                preferred_element_type=F32)
            s_ref[g] = s + tab_ref[o, g]
        for g in range(NA_GROUPS):
            s = s_ref[g]
            m = jnp.max(s, axis=-1, keepdims=True)
            p_ref[g] = jnp.exp(s - m).astype(BF16)
        for g in range(NA_GROUPS):
            ol = jnp.dot(p_ref[g], vaug_ref[pl.ds(start, band), 2 * g * LANES:(2 * g + 2) * LANES],
                         preferred_element_type=F32)
            acc = jnp.zeros((GRID_W, LANES), F32)
            for h in range(HEADS_PER_GROUP):
                rs = slice(h * GRID_W, (h + 1) * GRID_W)
                acc = jnp.where(lane_head == h, ol[rs, :LANES] / ol[rs, LANES:], acc)
            o_ref[0, pl.ds(qoff, GRID_W), g * LANES:(g + 1) * LANES] = acc.astype(BF16)
        return carry

    lax.fori_loop(0, rb_rows, row_body, 0)


def _natten(q, k, v, tab, rb_rows):
    bsz, seq, width = q.shape
    rows = seq // GRID_W
    nrb = rows // rb_rows
    blk = rb_rows * GRID_W
    band = WIN_R * GRID_W
    stacked = HEADS_PER_GROUP * GRID_W
    hmask = (np.arange(stacked)[:, None] // GRID_W == np.arange(LANES)[None, :] // NA_HEAD_DIM)
    cur = pl.BlockSpec((1, blk, width), lambda b, i: (b, i, 0))
    prv = pl.BlockSpec((1, blk, width), lambda b, i: (b, jnp.maximum(i - 1, 0), 0))
    nxt = pl.BlockSpec((1, blk, width), lambda b, i: (b, jnp.minimum(i + 1, nrb - 1), 0))
    return pl.pallas_call(
        functools.partial(_na_kernel, rows=rows, rb_rows=rb_rows),
        grid=(bsz, nrb),
        in_specs=[cur, prv, cur, nxt, prv, cur, nxt,
                  pl.BlockSpec(tab.shape, lambda b, i: (0, 0, 0, 0), pipeline_mode=pl.Buffered(1)),
                  _const_spec(hmask.shape)],
        out_specs=cur,
        out_shape=jax.ShapeDtypeStruct((bsz, seq, width), BF16),
        scratch_shapes=[pltpu.VMEM((3 * blk, width), BF16), pltpu.VMEM((3 * blk, 2 * width), BF16),
                        pltpu.VMEM((NA_GROUPS, stacked, band), F32),
                        pltpu.VMEM((NA_GROUPS, stacked, band), BF16)],
        compiler_params=_cparams(("parallel", "parallel")),
        name="natten",
    )(q, k, k, k, v, v, v, tab, jnp.asarray(hmask, BF16))


def _pack_bf16_pair(lo, hi):
    lo_bits = lax.bitcast_convert_type(lo.astype(BF16).astype(F32), jnp.int32)
    hi_bits = lax.bitcast_convert_type(hi.astype(BF16).astype(F32), jnp.int32)
    return lax.shift_right_logical(lo_bits, 16) | (hi_bits & HI_HALF_MASK)


def _unpack_bf16_pair(word):
    lo = lax.bitcast_convert_type(lax.shift_left(word, 16), F32)
    hi = lax.bitcast_convert_type(word & HI_HALF_MASK, F32)
    return lo, hi


def _route(logits_t, bias_col):
    e, t = logits_t.shape
    per_group = e // N_GROUP
    scores = _sigmoid(logits_t)
    biased = scores + bias_col
    b3 = biased.reshape(N_GROUP, per_group, t)
    idx3 = lax.broadcasted_iota(jnp.int32, b3.shape, 1)
    m1 = jnp.max(b3, axis=1, keepdims=True)
    first = jnp.min(jnp.where(b3 == m1, idx3, per_group), axis=1, keepdims=True)
    m2 = jnp.max(jnp.where(idx3 == first, -jnp.inf, b3), axis=1, keepdims=True)
    gscore = (m1 + m2).reshape(N_GROUP, t)
    gidx = lax.broadcasted_iota(jnp.int32, gscore.shape, 0)
    grank = jnp.zeros(gscore.shape, jnp.int32)
    for g in range(N_GROUP):
        other = gscore[g:g + 1, :]
        beats = (other > gscore) | ((other == gscore) & (g < gidx))
        grank = grank + beats.astype(jnp.int32)
    gsel = (grank < TOPK_GROUP).reshape(N_GROUP, 1, t)
    masked = jnp.where(gsel, b3, -jnp.inf).reshape(e, t)
    eidx = lax.broadcasted_iota(jnp.int32, masked.shape, 0)
    rank = jnp.zeros(masked.shape, jnp.int32)
    for j in range(e):
        other = masked[j:j + 1, :]
        beats = (other > masked) | ((other == masked) & (j < eidx))
        rank = rank + beats.astype(jnp.int32)
    sel = rank < TOP_K
    w = jnp.where(sel, scores, 0.0)
    return sel, w / jnp.sum(w, axis=0, keepdims=True) * ROUTED_SCALE


def _mix_kernel(x_ref, ca_ref, na_ref, sgc_ref, sgn_ref, wco_ref, wao_ref, wout_ref, g_ref,
                wr_ref, rb_ref, tri_ref, x1_ref, h2p_ref, te_ref, tw_ref, pos_ref, cnt_ref, base_ref):
    @pl.when(pl.program_id(0) == 0)
    def _():
        base_ref[...] = jnp.zeros_like(base_ref)

    yc = jnp.dot(ca_ref[...], wco_ref[...], preferred_element_type=F32)
    yn = jnp.dot(na_ref[...], wao_ref[...], preferred_element_type=F32)
    m = sgc_ref[...].astype(F32) * yc + sgn_ref[...].astype(F32) * yn
    x1 = x_ref[...] + jnp.dot(m.astype(BF16), wout_ref[...], preferred_element_type=F32)
    x1_ref[...] = x1
    h2 = _rms(x1, g_ref[...])
    h2p_ref[...] = _pack_bf16_pair(h2[:, :HALF_D], h2[:, HALF_D:])
    logits_t = lax.dot_general(wr_ref[...], h2, (((1,), (1,)), ((), ())),
                               preferred_element_type=F32, precision=lax.Precision.HIGHEST)
    sel, w = _route(logits_t, rb_ref[...])
    sel_b = jnp.where(sel, 1.0, 0.0).astype(BF16)
    pos = base_ref[:, 0:1] + jnp.dot(sel_b, tri_ref[...], preferred_element_type=F32)
    base_ref[...] = base_ref[...] + jnp.sum(sel_b.astype(F32), axis=1, keepdims=True)
    cnt_ref[...] = base_ref[...].astype(jnp.int32)
    ne = sel_b.shape[0]
    lower = jnp.where(lax.broadcasted_iota(jnp.int32, (ne, ne), 0) > lax.broadcasted_iota(jnp.int32, (ne, ne), 1),
                      1.0, 0.0).astype(BF16)
    slot = jnp.dot(lower, sel_b, preferred_element_type=F32)
    eidx = lax.broadcasted_iota(jnp.int32, sel.shape, 0).astype(F32)
    rows_e, rows_w, rows_p = [], [], []
    for k in range(TOP_K):
        mk = sel & (slot == float(k))
        pick = lambda a: jnp.sum(jnp.where(mk, a, 0.0), axis=0, keepdims=True)
        rows_e.append(pick(eidx))
        rows_w.append(pick(w))
        rows_p.append(pick(pos))
    te_ref[...] = jnp.concatenate(rows_e, axis=0).astype(jnp.int32)
    tw_ref[...] = jnp.concatenate(rows_w, axis=0)
    pos_ref[...] = jnp.concatenate(rows_p, axis=0).astype(jnp.int32)


def _mix_route(x, ca, na, sgc, sgn, wco, wao, wout, g_ffn, w_router_t, router_bias, tm):
    n = x.shape[0]
    row = lambda w: pl.BlockSpec((tm, w), lambda i: (i, 0))
    slots = pl.BlockSpec((TOP_K, tm), lambda i: (0, i))
    tri = (np.arange(tm)[:, None] < np.arange(tm)[None, :]).astype(np.float32)
    return pl.pallas_call(
        _mix_kernel,
        grid=(n // tm,),
        in_specs=[row(D_MODEL), row(CONV_CH), row(NA_WIDTH), row(D_MODEL), row(D_MODEL),
                  _const_spec(wco.shape), _const_spec(wao.shape), _const_spec(wout.shape),
                  _const_spec((1, D_MODEL)), _const_spec(w_router_t.shape), _const_spec((N_EXPERTS, 1)),
                  _const_spec((tm, tm))],
        out_specs=[row(D_MODEL), row(HALF_D), slots, slots, slots, _const_spec((N_EXPERTS, LANES))],
        out_shape=[jax.ShapeDtypeStruct((n, D_MODEL), F32), jax.ShapeDtypeStruct((n, HALF_D), jnp.int32),
                   jax.ShapeDtypeStruct((TOP_K, n), jnp.int32), jax.ShapeDtypeStruct((TOP_K, n), F32),
                   jax.ShapeDtypeStruct((TOP_K, n), jnp.int32),
                   jax.ShapeDtypeStruct((N_EXPERTS, LANES), jnp.int32)],
        scratch_shapes=[pltpu.VMEM((N_EXPERTS, LANES), F32)],
        compiler_params=_cparams(("arbitrary",)),
        name="mix_route",
    )(x, ca, na, sgc, sgn, wco, wao, wout, g_ffn, w_router_t, router_bias, jnp.asarray(tri, BF16))


def _dispatch_plan(top_e, pos, cnt, n_blocks):
    counts = cnt[:, 0]
    padded = ((counts + EXPERT_BLOCK - 1) // EXPERT_BLOCK) * EXPERT_BLOCK
    ends = jnp.cumsum(padded)
    starts = ends - padded
    experts = jnp.arange(N_EXPERTS, dtype=jnp.int32)
    start_of = jnp.sum(jnp.where(top_e[None] == experts[:, None, None], starts[:, None, None], 0), axis=0)
    dest = start_of + pos
    block_first_row = jnp.arange(n_blocks, dtype=jnp.int32) * EXPERT_BLOCK
    block_expert = jnp.minimum(jnp.sum(ends[None, :] <= block_first_row[:, None], axis=1),
                               N_EXPERTS - 1).astype(jnp.int32)
    n_used = (ends[-1:] // EXPERT_BLOCK).astype(jnp.int32)
    return dest.astype(jnp.int32), block_expert, n_used


def _sc_mesh():
    return plsc.VectorSubcoreMesh(core_axis_name="core", subcore_axis_name="subcore")


def _sc_worker_id(info):
    return lax.axis_index("subcore") * info.num_cores + lax.axis_index("core")


def _sc_dispatch(h2p, dest, n_rows):
    n, width = h2p.shape
    info = plsc.get_sparse_core_info()
    n_chunks = n // (info.num_cores * info.num_subcores * SC_CHUNK)
    dest3 = jnp.transpose(dest.reshape(TOP_K, n // SC_CHUNK, SC_CHUNK), (1, 0, 2))

    @functools.partial(
        pl.kernel, mesh=_sc_mesh(),
        out_type=jax.ShapeDtypeStruct((n_rows, width), h2p.dtype),
        scratch_types=[pltpu.VMEM((TOP_K, SC_CHUNK), jnp.int32), pltpu.VMEM((SC_CHUNK, width), h2p.dtype),
                       pltpu.SemaphoreType.DMA],
        name="moe_dispatch",
    )
    def dispatch(h_hbm, dest_hbm, xs_hbm, idx_v, rows_v, sem):
        wid = _sc_worker_id(info)

        @pl.loop(0, n_chunks)
        def _(i):
            c = wid * n_chunks + i
            pltpu.sync_copy(dest_hbm.at[c], idx_v)
            pltpu.sync_copy(h_hbm.at[pl.ds(c * SC_CHUNK, SC_CHUNK)], rows_v)
            copies = [pltpu.async_copy(rows_v, xs_hbm.at[idx_v.at[k]], sem) for k in range(TOP_K)]
            for cp in copies:
                cp.wait()

    return dispatch(h2p, dest3)


def _sc_combine(ys, dest):
    width = ys.shape[1]
    n_idx = dest.size
    info = plsc.get_sparse_core_info()
    per_worker = n_idx // (info.num_cores * info.num_subcores)
    n_chunks = per_worker // SC_CHUNK

    @functools.partial(
        pl.kernel, mesh=_sc_mesh(),
        out_type=jax.ShapeDtypeStruct((n_idx, width), ys.dtype),
        scratch_types=[pltpu.VMEM((SC_CHUNK,), jnp.int32), pltpu.VMEM((SC_CHUNK, width), ys.dtype),
                       pltpu.SemaphoreType.DMA],
        name="moe_combine",
    )
    def combine(ys_hbm, idx_hbm, yg_hbm, idx_v, rows_v, sem):
        base = _sc_worker_id(info) * per_worker

        @pl.loop(0, n_chunks)
        def _(i):
            off = base + i * SC_CHUNK
            pltpu.sync_copy(idx_hbm.at[pl.ds(off, SC_CHUNK)], idx_v)
            pltpu.async_copy(ys_hbm.at[idx_v], rows_v, sem).wait()
            pltpu.sync_copy(rows_v, yg_hbm.at[pl.ds(off, SC_CHUNK)])

    return combine(ys, dest.reshape(-1)).reshape(dest.shape + (width,))


def _experts_kernel(be_ref, nu_ref, xs_ref, wgu_ref, wd_ref, ys_ref):
    b = pl.program_id(0)

    @pl.when(b < nu_ref[0])
    def _():
        lo, hi = _unpack_bf16_pair(xs_ref[...])
        gu = (jnp.dot(lo.astype(BF16), wgu_ref[0, :HALF_D, :], preferred_element_type=F32)
              + jnp.dot(hi.astype(BF16), wgu_ref[0, HALF_D:, :], preferred_element_type=F32))
        g = gu[:, :EXPERT_FF]
        act = g * _sigmoid(g) * gu[:, EXPERT_FF:]
        y = jnp.dot(act.astype(BF16), wd_ref[0], preferred_element_type=F32)
        ys_ref[...] = _pack_bf16_pair(y[:, :HALF_D], y[:, HALF_D:])

    @pl.when(b >= nu_ref[0])
    def _():
        ys_ref[...] = jnp.zeros_like(ys_ref)


def _experts(xs, block_expert, n_used, wgu, wd):
    n_rows, width = xs.shape
    rows = pl.BlockSpec((EXPERT_BLOCK, width), lambda b, be, nu: (b, 0))
    return pl.pallas_call(
        _experts_kernel,
        grid_spec=pltpu.PrefetchScalarGridSpec(
            num_scalar_prefetch=2,
            grid=(n_rows // EXPERT_BLOCK,),
            in_specs=[rows,
                      pl.BlockSpec((1,) + wgu.shape[1:], lambda b, be, nu: (be[b], 0, 0)),
                      pl.BlockSpec((1,) + wd.shape[1:], lambda b, be, nu: (be[b], 0, 0))],
            out_specs=rows,
        ),
        out_shape=jax.ShapeDtypeStruct((n_rows, width), jnp.int32),
        compiler_params=_cparams(("parallel",)),
        name="moe_experts",
    )(block_expert, n_used, xs, wgu, wd)


def _tail_kernel(x1_ref, yg_ref, tw_ref, h2p_ref, p_ref, wsg_ref, wsd_ref, gp_ref, wpg_ref, wpp_ref, gf_ref,
                 o_ref, *, final_norm):
    ylo = yhi = None
    for k in range(TOP_K):
        lo, hi = _unpack_bf16_pair(yg_ref[k])
        wk = tw_ref[:, k:k + 1]
        ylo = wk * lo if ylo is None else ylo + wk * lo
        yhi = wk * hi if yhi is None else yhi + wk * hi
    hlo, hhi = _unpack_bf16_pair(h2p_ref[...])
    gu = (jnp.dot(hlo.astype(BF16), wsg_ref[:HALF_D, :], preferred_element_type=F32)
          + jnp.dot(hhi.astype(BF16), wsg_ref[HALF_D:, :], preferred_element_type=F32))
    g = gu[:, :SHARED_FF]
    act = g * _sigmoid(g) * gu[:, SHARED_FF:]
    sh = jnp.dot(act.astype(BF16), wsd_ref[...], preferred_element_type=F32)
    x2 = x1_ref[...] + jnp.concatenate([ylo, yhi], axis=-1) + sh
    h3 = _rms(x2, gp_ref[...]).astype(BF16)
    gate = _sigmoid(jnp.dot(h3, wpg_ref[...], preferred_element_type=F32))
    pp = jnp.dot(p_ref[...].astype(BF16), wpp_ref[...], preferred_element_type=F32)
    x3 = x2 + pp * gate
    o_ref[...] = _rms(x3, gf_ref[...]) if final_norm else x3


def _tail(x1, yg, tw, h2p, p, wsg, wsd, g_ple, wpg, wpp, g_final, tm, final_norm):
    n = x1.shape[0]
    row = lambda w: pl.BlockSpec((tm, w), lambda i: (i, 0))
    return pl.pallas_call(
        functools.partial(_tail_kernel, final_norm=final_norm),
        grid=(n // tm,),
        in_specs=[row(D_MODEL), pl.BlockSpec((TOP_K, tm, HALF_D), lambda i: (0, i, 0)), row(TOP_K),
                  row(HALF_D), row(PLE_DIM),
                  _const_spec(wsg.shape), _const_spec(wsd.shape), _const_spec((1, D_MODEL)),
                  _const_spec(wpg.shape), _const_spec(wpp.shape), _const_spec((1, D_MODEL))],
        out_specs=row(D_MODEL),
        out_shape=jax.ShapeDtypeStruct((n, D_MODEL), F32),
        compiler_params=_cparams(("parallel",)),
        name="tail",
    )(x1, yg, tw, h2p, p, wsg, wsd, g_ple, wpg, wpp, g_final)


def _tile(n, pref):
    t = min(pref, n)
    while n % t:
        t //= 2
    return t


def _encoder_group(x, p, w, final_norm):
    bsz, seq, d = x.shape
    n = bsz * seq
    xf = x.reshape(n, d)
    tm = _tile(n, 512)
    glu, q, k, v, sgc, sgn = _in_proj(xf, w["g_mix"], w["w_in"], w["b_in"], tm)
    ca = _conv_branch(glu.reshape(bsz, seq, CONV_CH), w["conv_w"], w["conv_b"], w["conv_ln_g"],
                      w["conv_ln_b"], _tile(seq, 512))
    b3 = lambda a: a.reshape(bsz, seq, NA_WIDTH)
    na = _natten(b3(q), b3(k), b3(v), w["na_tab"], _tile(seq // GRID_W, 8))
    x1, h2p, top_e, top_w, pos, cnt = _mix_route(
        xf, ca.reshape(n, CONV_CH), na.reshape(n, NA_WIDTH), sgc, sgn, w["w_conv_out"], w["w_attn_out"],
        w["w_out"], w["g_ffn"], w["w_router_t"], w["router_bias"], tm)
    n_blocks = n * TOP_K // EXPERT_BLOCK + N_EXPERTS
    dest, block_expert, n_used = _dispatch_plan(top_e, pos, cnt, n_blocks)
    xs = _sc_dispatch(h2p, dest, n_blocks * EXPERT_BLOCK)
    ys = _experts(xs, block_expert, n_used, w["w_exp_gu"], w["w_exp_down"])
    yg = _sc_combine(ys, dest)
    out = _tail(x1, yg, top_w.T, h2p, p.reshape(n, PLE_DIM), w["w_sh_gu"], w["w_sh_down"], w["g_ple"],
                w["w_ple_gate"], w["w_ple_proj"], w["g_final"], _tile(n, 256), final_norm)
    return out.reshape(bsz, seq, d)


def kernel(x_prompt, x_sample, p_prompt, p_sample, g_mix, w_in, b_in, conv_w, conv_b, conv_ln_g,
           conv_ln_b, w_conv_out, rpb, w_attn_out, w_out, g_ffn, w_router, router_bias, w_exp_gu,
           w_exp_down, w_sh_gu, w_sh_down, g_ple, w_ple_gate, w_ple_proj, g_final):
    depth = g_mix.shape[0]
    xp, xs = x_prompt, x_sample
    for i in range(depth):
        row = lambda a: a[i].reshape(1, -1).astype(F32)
        w = {
            "g_mix": row(g_mix), "w_in": w_in[i].astype(BF16), "b_in": row(b_in),
            "conv_w": conv_w[i].astype(F32), "conv_b": row(conv_b),
            "conv_ln_g": row(conv_ln_g), "conv_ln_b": row(conv_ln_b),
            "w_conv_out": w_conv_out[i].astype(BF16), "na_tab": _na_bias_table(rpb[i]),
            "w_attn_out": w_attn_out[i].astype(BF16), "w_out": w_out[i].astype(BF16),
            "g_ffn": row(g_ffn), "w_router_t": w_router[i].T.astype(F32),
            "router_bias": router_bias[i].reshape(-1, 1).astype(F32),
            "w_exp_gu": w_exp_gu[i].astype(BF16), "w_exp_down": w_exp_down[i].astype(BF16),
            "w_sh_gu": w_sh_gu[i].astype(BF16), "w_sh_down": w_sh_down[i].astype(BF16),
            "g_ple": row(g_ple), "w_ple_gate": w_ple_gate[i].astype(BF16),
            "w_ple_proj": w_ple_proj[i].astype(BF16), "g_final": g_final.reshape(1, -1).astype(F32),
        }
        xp = _encoder_group(xp, p_prompt[i], w, i == depth - 1)
        xs = _encoder_group(xs, p_sample[i], w, i == depth - 1)
    return (xp, xs)
```

```python
import functools

import jax
import jax.numpy as jnp
import numpy as np
from jax import lax
from jax.experimental import pallas as pl
from jax.experimental.pallas import tpu as pltpu
from jax.experimental.pallas import tpu_sc as plsc

D_MODEL = 1024
GRID_W = 64
CONV_CH = 512
CONV_K = 31
NA_HEADS = 16
NA_HEAD_DIM = 32
NA_WIDTH = NA_HEADS * NA_HEAD_DIM
WIN_R = 8
WIN_C = 16
N_EXPERTS = 64
TOP_K = 8
N_GROUP = 8
TOPK_GROUP = 4
EXPERT_FF = 256
SHARED_FF = 256
ROUTED_SCALE = 2.5
PLE_DIM = 256
EPS = 1e-6
NEG_BIG = -1e30

OFF_Q = 2 * CONV_CH
OFF_K = OFF_Q + NA_WIDTH
OFF_V = OFF_K + NA_WIDTH
OFF_GATE = OFF_V + NA_WIDTH

CONV_HALO = 16
CONV_ROWS = 32
VMEM_LIMIT = 56 * 1024 * 1024
LANES = 128
SUBLANES = 8
HEADS_PER_GROUP = LANES // NA_HEAD_DIM
NA_GROUPS = NA_HEADS // HEADS_PER_GROUP
HALF_D = D_MODEL // 2
HI_HALF_MASK = -65536
EXPERT_BLOCK = 512
SC_CHUNK = 64

BF16 = jnp.bfloat16
F32 = jnp.float32


def _cparams(sem):
    return pltpu.CompilerParams(dimension_semantics=sem, vmem_limit_bytes=VMEM_LIMIT)


def _const_spec(shape):
    nd = len(shape)
    return pl.BlockSpec(shape, lambda *_: (0,) * nd)


def _rms(xf, g):
    return xf * lax.rsqrt(jnp.mean(xf * xf, axis=-1, keepdims=True) + EPS) * g


def _sigmoid(x):
    return 1.0 / (1.0 + jnp.exp(-x))


def _in_proj_kernel(x_ref, g_ref, w_ref, b_ref, glu_ref, q_ref, k_ref, v_ref, sgc_ref, sgn_ref):
    h = _rms(x_ref[...], g_ref[...]).astype(BF16)

    def proj(lo, hi):
        return jnp.dot(h, w_ref[:, lo:hi], preferred_element_type=F32) + b_ref[:, lo:hi]

    u = proj(0, CONV_CH)
    gate = proj(CONV_CH, 2 * CONV_CH)
    glu_ref[...] = (u * _sigmoid(gate)).astype(BF16)
    q_ref[...] = (proj(OFF_Q, OFF_K) * (NA_HEAD_DIM ** -0.5)).astype(BF16)
    k_ref[...] = proj(OFF_K, OFF_V).astype(BF16)
    v_ref[...] = proj(OFF_V, OFF_GATE).astype(BF16)
    for c in range(2):
        lo = OFF_GATE + c * CONV_CH
        sgc_ref[:, c * CONV_CH:(c + 1) * CONV_CH] = _sigmoid(proj(lo, lo + CONV_CH)).astype(BF16)
        lo = OFF_GATE + D_MODEL + c * CONV_CH
        sgn_ref[:, c * CONV_CH:(c + 1) * CONV_CH] = _sigmoid(proj(lo, lo + CONV_CH)).astype(BF16)


def _in_proj(x, g_mix, w_in, b_in, tm):
    n = x.shape[0]
    cols = w_in.shape[1]
    row = lambda w: pl.BlockSpec((tm, w), lambda i: (i, 0))
    out_w = (CONV_CH, NA_WIDTH, NA_WIDTH, NA_WIDTH, D_MODEL, D_MODEL)
    return pl.pallas_call(
        _in_proj_kernel,
        grid=(n // tm,),
        in_specs=[row(D_MODEL), _const_spec((1, D_MODEL)), _const_spec((D_MODEL, cols)),
                  _const_spec((1, cols))],
        out_specs=[row(w) for w in out_w],
        out_shape=[jax.ShapeDtypeStruct((n, w), BF16) for w in out_w],
        compiler_params=_cparams(("parallel",)),
        name="in_proj",
    )(x, g_mix, w_in, b_in)


def _conv_kernel(prev_ref, cur_ref, next_ref, w_ref, b_ref, lg_ref, lb_ref, o_ref, ext_ref, *, ts):
    i = pl.program_id(1)
    last = pl.num_programs(1) - 1
    prev = prev_ref[0].astype(F32)
    nxt = next_ref[0].astype(F32)
    ext_ref[0, 0:CONV_HALO, :] = jnp.where(i > 0, prev, 0.0)
    ext_ref[0, CONV_HALO:CONV_HALO + ts, :] = cur_ref[0].astype(F32)
    ext_ref[0, CONV_HALO + ts:, :] = jnp.where(i < last, nxt, 0.0)
    span = ts + 2 * CONV_HALO - SUBLANES
    for s in range(1, SUBLANES):
        ext_ref[s, 0:span, :] = ext_ref[0, s:s + span, :]
    base = CONV_HALO - CONV_K // 2
    for c0 in range(0, ts, CONV_ROWS):
        acc = jnp.zeros((CONV_ROWS, CONV_CH), F32)
        for j in range(CONV_K):
            shift = (base + j) % SUBLANES
            lo = base + j - shift + c0
            acc = acc + ext_ref[shift, lo:lo + CONV_ROWS, :] * w_ref[j:j + 1, :]
        y = acc + b_ref[...]
        mu = jnp.mean(y, axis=-1, keepdims=True)
        yc = y - mu
        var = jnp.mean(yc * yc, axis=-1, keepdims=True)
        z = yc * lax.rsqrt(var + EPS) * lg_ref[...] + lb_ref[...]
        o_ref[0, c0:c0 + CONV_ROWS, :] = (z * _sigmoid(z)).astype(BF16)


def _conv_branch(glu, conv_w, conv_b, ln_g, ln_b, ts):
    bsz, seq, ch = glu.shape
    nh = ts // CONV_HALO
    n_halo = seq // CONV_HALO
    return pl.pallas_call(
        functools.partial(_conv_kernel, ts=ts),
        grid=(bsz, seq // ts),
        in_specs=[
            pl.BlockSpec((1, CONV_HALO, ch), lambda b, i: (b, jnp.maximum(i * nh - 1, 0), 0)),
            pl.BlockSpec((1, ts, ch), lambda b, i: (b, i, 0)),
            pl.BlockSpec((1, CONV_HALO, ch), lambda b, i: (b, jnp.minimum((i + 1) * nh, n_halo - 1), 0)),
            _const_spec((CONV_K, ch)), _const_spec((1, ch)), _const_spec((1, ch)), _const_spec((1, ch)),
        ],
        out_specs=pl.BlockSpec((1, ts, ch), lambda b, i: (b, i, 0)),
        out_shape=jax.ShapeDtypeStruct((bsz, seq, ch), BF16),
        scratch_shapes=[pltpu.VMEM((SUBLANES, ts + 2 * CONV_HALO, ch), F32)],
        compiler_params=_cparams(("parallel", "parallel")),
        name="conv_branch",
    )(glu, glu, glu, conv_w, conv_b, ln_g, ln_b)


def _na_bias_table(rpb):
    col = np.arange(GRID_W)
    col_start = np.clip(col - WIN_C // 2, 0, GRID_W - WIN_C)
    kc = np.arange(GRID_W)
    inside = (kc[None, :] >= col_start[:, None]) & (kc[None, :] < col_start[:, None] + WIN_C)
    dc = kc[None, :] - col[:, None] + (WIN_C - 1)
    n_dc = 2 * WIN_C - 1
    onehot = (inside[:, :, None] & (dc[:, :, None] == np.arange(n_dc))).astype(np.float32)
    t = jnp.einsum("hrd,ckd->hrck", rpb.astype(F32), jnp.asarray(onehot),
                   precision=lax.Precision.HIGHEST)
    t = jnp.where(jnp.asarray(inside)[None, None], t, NEG_BIG)
    t = jnp.stack([t[:, o:o + WIN_R] for o in range(WIN_R)], axis=0)
    t = jnp.transpose(t, (0, 1, 3, 2, 4))
    return t.reshape(WIN_R, NA_GROUPS, HEADS_PER_GROUP * GRID_W, WIN_R * GRID_W)


def _na_kernel(q_ref, kp_ref, kc_ref, kn_ref, vp_ref, vc_ref, vn_ref, tab_ref, hmask_ref, o_ref,
               kband_ref, vaug_ref, s_ref, p_ref, *, rows, rb_rows):
    blk = rb_rows * GRID_W
    rb = pl.program_id(1)
    kband_ref[0:blk, :] = kp_ref[0]
    kband_ref[blk:2 * blk, :] = kc_ref[0]
    kband_ref[2 * blk:, :] = kn_ref[0]
    ones = jnp.ones((blk, LANES), BF16)
    for part, v_ref in enumerate((vp_ref, vc_ref, vn_ref)):
        rs = slice(part * blk, (part + 1) * blk)
        for g in range(NA_GROUPS):
            vaug_ref[rs, 2 * g * LANES:(2 * g + 1) * LANES] = v_ref[0, :, g * LANES:(g + 1) * LANES]
            vaug_ref[rs, (2 * g + 1) * LANES:(2 * g + 2) * LANES] = ones
    band = WIN_R * GRID_W
    lane_head = lax.broadcasted_iota(jnp.int32, (GRID_W, LANES), 1) // NA_HEAD_DIM

    def row_body(i, carry):
        r = rb * rb_rows + i
        r0 = jnp.clip(r - WIN_R // 2, 0, rows - WIN_R)
        start = pl.multiple_of((r0 - (rb - 1) * rb_rows) * GRID_W, GRID_W)
        o = r0 - r + (WIN_R - 1)
        qoff = pl.multiple_of(i * GRID_W, GRID_W)
        qi = q_ref[0, pl.ds(qoff, GRID_W), :]
        for g in range(NA_GROUPS):
            gl = slice(g * LANES, (g + 1) * LANES)
            q4 = jnp.concatenate([qi[:, gl]] * HEADS_PER_GROUP, axis=0) * hmask_ref[...]
            s = lax.dot_general(q4, kband_ref[pl.ds(start, band), gl], (((1,), (1,)), ((), ())),
                                preferred_element_type=F32)
            s_ref[g] = s + tab_ref[o, g]
        for g in range(NA_GROUPS):
            s = s_ref[g]
            m = jnp.max(s, axis=-1, keepdims=True)
            p_ref[g] = jnp.exp(s - m).astype(BF16)
        for g in range(NA_GROUPS):
            ol = jnp.dot(p_ref[g], vaug_ref[pl.ds(start, band), 2 * g * LANES:(2 * g + 2) * LANES],
                         preferred_element_type=F32)
            acc = jnp.zeros((GRID_W, LANES), F32)
            for h in range(HEADS_PER_GROUP):
                rs = slice(h * GRID_W, (h + 1) * GRID_W)
                acc = jnp.where(lane_head == h, ol[rs, :LANES] / ol[rs, LANES:], acc)
            o_ref[0, pl.ds(qoff, GRID_W), g * LANES:(g + 1) * LANES] = acc.astype(BF16)
        return carry

    lax.fori_loop(0, rb_rows, row_body, 0)


def _natten(q, k, v, tab, rb_rows):
    bsz, seq, width = q.shape
    rows = seq // GRID_W
    nrb = rows // rb_rows
    blk = rb_rows * GRID_W
    band = WIN_R * GRID_W
    stacked = HEADS_PER_GROUP * GRID_W
    hmask = (np.arange(stacked)[:, None] // GRID_W == np.arange(LANES)[None, :] // NA_HEAD_DIM)
    cur = pl.BlockSpec((1, blk, width), lambda b, i: (b, i, 0))
    prv = pl.BlockSpec((1, blk, width), lambda b, i: (b, jnp.maximum(i - 1, 0), 0))
    nxt = pl.BlockSpec((1, blk, width), lambda b, i: (b, jnp.minimum(i + 1, nrb - 1), 0))
    return pl.pallas_call(
        functools.partial(_na_kernel, rows=rows, rb_rows=rb_rows),
        grid=(bsz, nrb),
        in_specs=[cur, prv, cur, nxt, prv, cur, nxt,
                  pl.BlockSpec(tab.shape, lambda b, i: (0, 0, 0, 0), pipeline_mode=pl.Buffered(1)),
                  _const_spec(hmask.shape)],
        out_specs=cur,
        out_shape=jax.ShapeDtypeStruct((bsz, seq, width), BF16),
        scratch_shapes=[pltpu.VMEM((3 * blk, width), BF16), pltpu.VMEM((3 * blk, 2 * width), BF16),
                        pltpu.VMEM((NA_GROUPS, stacked, band), F32),
                        pltpu.VMEM((NA_GROUPS, stacked, band), BF16)],
        compiler_params=_cparams(("parallel", "parallel")),
        name="natten",
    )(q, k, k, k, v, v, v, tab, jnp.asarray(hmask, BF16))


def _pack_bf16_pair(lo, hi):
    lo_bits = lax.bitcast_convert_type(lo.astype(BF16).astype(F32), jnp.int32)
    hi_bits = lax.bitcast_convert_type(hi.astype(BF16).astype(F32), jnp.int32)
    return lax.shift_right_logical(lo_bits, 16) | (hi_bits & HI_HALF_MASK)


def _unpack_bf16_pair(word):
    lo = lax.bitcast_convert_type(lax.shift_left(word, 16), F32)
    hi = lax.bitcast_convert_type(word & HI_HALF_MASK, F32)
    return lo, hi


def _route(logits_t, bias_col):
    e, t = logits_t.shape
    per_group = e // N_GROUP
    scores = _sigmoid(logits_t)
    biased = scores + bias_col
    b3 = biased.reshape(N_GROUP, per_group, t)
    idx3 = lax.broadcasted_iota(jnp.int32, b3.shape, 1)
    m1 = jnp.max(b3, axis=1, keepdims=True)
    first = jnp.min(jnp.where(b3 == m1, idx3, per_group), axis=1, keepdims=True)
    m2 = jnp.max(jnp.where(idx3 == first, -jnp.inf, b3), axis=1, keepdims=True)
    gscore = (m1 + m2).reshape(N_GROUP, t)
    gidx = lax.broadcasted_iota(jnp.int32, gscore.shape, 0)
    grank = jnp.zeros(gscore.shape, jnp.int32)
    for g in range(N_GROUP):
        other = gscore[g:g + 1, :]
        beats = (other > gscore) | ((other == gscore) & (g < gidx))
        grank = grank + beats.astype(jnp.int32)
    gsel = (grank < TOPK_GROUP).reshape(N_GROUP, 1, t)
    masked = jnp.where(gsel, b3, -jnp.inf).reshape(e, t)
    sub = 8
    blocks = [masked[r:r + sub, :] for r in range(0, e, sub)]
    ranks = [jnp.zeros((sub, t), F32) for _ in blocks]
    row_in_block = lax.broadcasted_iota(jnp.int32, (sub, t), 0)
    for j in range(e):
        other = masked[j:j + 1, :]
        for bi, blk in enumerate(blocks):
            r0 = bi * sub
            if r0 > j:
                beats = other >= blk
            elif r0 + sub - 1 <= j:
                beats = other > blk
            else:
                beats = (other > blk) | ((other == blk) & (row_in_block > j - r0))
            ranks[bi] = ranks[bi] + jnp.where(beats, 1.0, 0.0)
    sel = jnp.concatenate(ranks, axis=0) < float(TOP_K)
    w = jnp.where(sel, scores, 0.0)
    return sel, w / jnp.sum(w, axis=0, keepdims=True) * ROUTED_SCALE


def _mix_kernel(x_ref, ca_ref, na_ref, sgc_ref, sgn_ref, wco_ref, wao_ref, wout_ref, g_ref,
                wr_ref, rb_ref, tri_ref, x1_ref, h2p_ref, te_ref, tw_ref, pos_ref, cnt_ref, base_ref):
    @pl.when(pl.program_id(0) == 0)
    def _():
        base_ref[...] = jnp.zeros_like(base_ref)

    yc = jnp.dot(ca_ref[...], wco_ref[...], preferred_element_type=F32)
    yn = jnp.dot(na_ref[...], wao_ref[...], preferred_element_type=F32)
    m = sgc_ref[...].astype(F32) * yc + sgn_ref[...].astype(F32) * yn
    x1 = x_ref[...] + jnp.dot(m.astype(BF16), wout_ref[...], preferred_element_type=F32)
    x1_ref[...] = x1
    h2 = _rms(x1, g_ref[...])
    h2p_ref[...] = _pack_bf16_pair(h2[:, :HALF_D], h2[:, HALF_D:])
    logits_t = lax.dot_general(wr_ref[...], h2, (((1,), (1,)), ((), ())),
                               preferred_element_type=F32, precision=lax.Precision.HIGHEST)
    sel, w = _route(logits_t, rb_ref[...])
    sel_b = jnp.where(sel, 1.0, 0.0).astype(BF16)
    pos = base_ref[:, 0:1] + jnp.dot(sel_b, tri_ref[...], preferred_element_type=F32)
    base_ref[...] = base_ref[...] + jnp.sum(sel_b.astype(F32), axis=1, keepdims=True)
    cnt_ref[...] = base_ref[...].astype(jnp.int32)
    ne = sel_b.shape[0]
    lower = jnp.where(lax.broadcasted_iota(jnp.int32, (ne, ne), 0) > lax.broadcasted_iota(jnp.int32, (ne, ne), 1),
                      1.0, 0.0).astype(BF16)
    slot = jnp.dot(lower, sel_b, preferred_element_type=F32)
    eidx = lax.broadcasted_iota(jnp.int32, sel.shape, 0).astype(F32)
    rows_e, rows_w, rows_p = [], [], []
    for k in range(TOP_K):
        mk = sel & (slot == float(k))
        pick = lambda a: jnp.sum(jnp.where(mk, a, 0.0), axis=0, keepdims=True)
        rows_e.append(pick(eidx))
        rows_w.append(pick(w))
        rows_p.append(pick(pos))
    te_ref[...] = jnp.concatenate(rows_e, axis=0).astype(jnp.int32)
    tw_ref[...] = jnp.concatenate(rows_w, axis=0)
    pos_ref[...] = jnp.concatenate(rows_p, axis=0).astype(jnp.int32)


def _mix_route(x, ca, na, sgc, sgn, wco, wao, wout, g_ffn, w_router_t, router_bias, tm):
    n = x.shape[0]
    row = lambda w: pl.BlockSpec((tm, w), lambda i: (i, 0))
    slots = pl.BlockSpec((TOP_K, tm), lambda i: (0, i))
    tri = (np.arange(tm)[:, None] < np.arange(tm)[None, :]).astype(np.float32)
    return pl.pallas_call(
        _mix_kernel,
        grid=(n // tm,),
        in_specs=[row(D_MODEL), row(CONV_CH), row(NA_WIDTH), row(D_MODEL), row(D_MODEL),
                  _const_spec(wco.shape), _const_spec(wao.shape), _const_spec(wout.shape),
                  _const_spec((1, D_MODEL)), _const_spec(w_router_t.shape), _const_spec((N_EXPERTS, 1)),
                  _const_spec((tm, tm))],
        out_specs=[row(D_MODEL), row(HALF_D), slots, slots, slots, _const_spec((N_EXPERTS, LANES))],
        out_shape=[jax.ShapeDtypeStruct((n, D_MODEL), F32), jax.ShapeDtypeStruct((n, HALF_D), jnp.int32),
                   jax.ShapeDtypeStruct((TOP_K, n), jnp.int32), jax.ShapeDtypeStruct((TOP_K, n), F32),
                   jax.ShapeDtypeStruct((TOP_K, n), jnp.int32),
                   jax.ShapeDtypeStruct((N_EXPERTS, LANES), jnp.int32)],
        scratch_shapes=[pltpu.VMEM((N_EXPERTS, LANES), F32)],
        compiler_params=_cparams(("arbitrary",)),
        name="mix_route",
    )(x, ca, na, sgc, sgn, wco, wao, wout, g_ffn, w_router_t, router_bias, jnp.asarray(tri, BF16))


def _dispatch_plan(top_e, pos, cnt, n_blocks):
    counts = cnt[:, 0]
    padded = ((counts + EXPERT_BLOCK - 1) // EXPERT_BLOCK) * EXPERT_BLOCK
    ends = jnp.cumsum(padded)
    starts = ends - padded
    experts = jnp.arange(N_EXPERTS, dtype=jnp.int32)
    start_of = jnp.sum(jnp.where(top_e[None] == experts[:, None, None], starts[:, None, None], 0), axis=0)
    dest = start_of + pos
    block_first_row = jnp.arange(n_blocks, dtype=jnp.int32) * EXPERT_BLOCK
    block_expert = jnp.minimum(jnp.sum(ends[None, :] <= block_first_row[:, None], axis=1),
                               N_EXPERTS - 1).astype(jnp.int32)
    n_used = (ends[-1:] // EXPERT_BLOCK).astype(jnp.int32)
    return dest.astype(jnp.int32), block_expert, n_used


def _sc_mesh():
    return plsc.VectorSubcoreMesh(core_axis_name="core", subcore_axis_name="subcore")


def _sc_worker_id(info):
    return lax.axis_index("subcore") * info.num_cores + lax.axis_index("core")


def _sc_dispatch(h2p, dest, n_rows):
    n, width = h2p.shape
    info = plsc.get_sparse_core_info()
    n_chunks = n // (info.num_cores * info.num_subcores * SC_CHUNK)
    dest3 = jnp.transpose(dest.reshape(TOP_K, n // SC_CHUNK, SC_CHUNK), (1, 0, 2))

    @functools.partial(
        pl.kernel, mesh=_sc_mesh(),
        out_type=jax.ShapeDtypeStruct((n_rows, width), h2p.dtype),
        scratch_types=[pltpu.VMEM((TOP_K, SC_CHUNK), jnp.int32), pltpu.VMEM((SC_CHUNK, width), h2p.dtype),
                       pltpu.SemaphoreType.DMA],
        name="moe_dispatch",
    )
    def dispatch(h_hbm, dest_hbm, xs_hbm, idx_v, rows_v, sem):
        wid = _sc_worker_id(info)

        @pl.loop(0, n_chunks)
        def _(i):
            c = wid * n_chunks + i
            pltpu.sync_copy(dest_hbm.at[c], idx_v)
            pltpu.sync_copy(h_hbm.at[pl.ds(c * SC_CHUNK, SC_CHUNK)], rows_v)
            copies = [pltpu.async_copy(rows_v, xs_hbm.at[idx_v.at[k]], sem) for k in range(TOP_K)]
            for cp in copies:
                cp.wait()

    return dispatch(h2p, dest3)


def _sc_combine(ys, dest):
    width = ys.shape[1]
    n_idx = dest.size
    info = plsc.get_sparse_core_info()
    per_worker = n_idx // (info.num_cores * info.num_subcores)
    n_chunks = per_worker // SC_CHUNK

    @functools.partial(
        pl.kernel, mesh=_sc_mesh(),
        out_type=jax.ShapeDtypeStruct((n_idx, width), ys.dtype),
        scratch_types=[pltpu.VMEM((SC_CHUNK,), jnp.int32), pltpu.VMEM((SC_CHUNK, width), ys.dtype),
                       pltpu.SemaphoreType.DMA],
        name="moe_combine",
    )
    def combine(ys_hbm, idx_hbm, yg_hbm, idx_v, rows_v, sem):
        base = _sc_worker_id(info) * per_worker

        @pl.loop(0, n_chunks)
        def _(i):
            off = base + i * SC_CHUNK
            pltpu.sync_copy(idx_hbm.at[pl.ds(off, SC_CHUNK)], idx_v)
            pltpu.async_copy(ys_hbm.at[idx_v], rows_v, sem).wait()
            pltpu.sync_copy(rows_v, yg_hbm.at[pl.ds(off, SC_CHUNK)])

    return combine(ys, dest.reshape(-1)).reshape(dest.shape + (width,))


def _experts_kernel(be_ref, nu_ref, xs_ref, wgu_ref, wd_ref, ys_ref):
    b = pl.program_id(0)

    @pl.when(b < nu_ref[0])
    def _():
        lo, hi = _unpack_bf16_pair(xs_ref[...])
        gu = (jnp.dot(lo.astype(BF16), wgu_ref[0, :HALF_D, :], preferred_element_type=F32)
              + jnp.dot(hi.astype(BF16), wgu_ref[0, HALF_D:, :], preferred_element_type=F32))
        g = gu[:, :EXPERT_FF]
        act = g * _sigmoid(g) * gu[:, EXPERT_FF:]
        y = jnp.dot(act.astype(BF16), wd_ref[0], preferred_element_type=F32)
        ys_ref[...] = _pack_bf16_pair(y[:, :HALF_D], y[:, HALF_D:])

    @pl.when(b >= nu_ref[0])
    def _():
        ys_ref[...] = jnp.zeros_like(ys_ref)


def _experts(xs, block_expert, n_used, wgu, wd):
    n_rows, width = xs.shape
    rows = pl.BlockSpec((EXPERT_BLOCK, width), lambda b, be, nu: (b, 0))
    return pl.pallas_call(
        _experts_kernel,
        grid_spec=pltpu.PrefetchScalarGridSpec(
            num_scalar_prefetch=2,
            grid=(n_rows // EXPERT_BLOCK,),
            in_specs=[rows,
                      pl.BlockSpec((1,) + wgu.shape[1:], lambda b, be, nu: (be[b], 0, 0)),
                      pl.BlockSpec((1,) + wd.shape[1:], lambda b, be, nu: (be[b], 0, 0))],
            out_specs=rows,
        ),
        out_shape=jax.ShapeDtypeStruct((n_rows, width), jnp.int32),
        compiler_params=_cparams(("parallel",)),
        name="moe_experts",
    )(block_expert, n_used, xs, wgu, wd)


def _tail_kernel(x1_ref, yg_ref, tw_ref, h2p_ref, p_ref, wsg_ref, wsd_ref, gp_ref, wpg_ref, wpp_ref, gf_ref,
                 o_ref, *, final_norm):
    ylo = yhi = None
    for k in range(TOP_K):
        lo, hi = _unpack_bf16_pair(yg_ref[k])
        wk = tw_ref[:, k:k + 1]
        ylo = wk * lo if ylo is None else ylo + wk * lo
        yhi = wk * hi if yhi is None else yhi + wk * hi
    hlo, hhi = _unpack_bf16_pair(h2p_ref[...])
    gu = (jnp.dot(hlo.astype(BF16), wsg_ref[:HALF_D, :], preferred_element_type=F32)
          + jnp.dot(hhi.astype(BF16), wsg_ref[HALF_D:, :], preferred_element_type=F32))
    g = gu[:, :SHARED_FF]
    act = g * _sigmoid(g) * gu[:, SHARED_FF:]
    sh = jnp.dot(act.astype(BF16), wsd_ref[...], preferred_element_type=F32)
    x2 = x1_ref[...] + jnp.concatenate([ylo, yhi], axis=-1) + sh
    h3 = _rms(x2, gp_ref[...]).astype(BF16)
    gate = _sigmoid(jnp.dot(h3, wpg_ref[...], preferred_element_type=F32))
    pp = jnp.dot(p_ref[...].astype(BF16), wpp_ref[...], preferred_element_type=F32)
    x3 = x2 + pp * gate
    o_ref[...] = _rms(x3, gf_ref[...]) if final_norm else x3


def _tail(x1, yg, tw, h2p, p, wsg, wsd, g_ple, wpg, wpp, g_final, tm, final_norm):
    n = x1.shape[0]
    row = lambda w: pl.BlockSpec((tm, w), lambda i: (i, 0))
    return pl.pallas_call(
        functools.partial(_tail_kernel, final_norm=final_norm),
        grid=(n // tm,),
        in_specs=[row(D_MODEL), pl.BlockSpec((TOP_K, tm, HALF_D), lambda i: (0, i, 0)), row(TOP_K),
                  row(HALF_D), row(PLE_DIM),
                  _const_spec(wsg.shape), _const_spec(wsd.shape), _const_spec((1, D_MODEL)),
                  _const_spec(wpg.shape), _const_spec(wpp.shape), _const_spec((1, D_MODEL))],
        out_specs=row(D_MODEL),
        out_shape=jax.ShapeDtypeStruct((n, D_MODEL), F32),
        compiler_params=_cparams(("parallel",)),
        name="tail",
    )(x1, yg, tw, h2p, p, wsg, wsd, g_ple, wpg, wpp, g_final)


def _tile(n, pref):
    t = min(pref, n)
    while n % t:
        t //= 2
    return t


def _encoder_group(x, p, w, final_norm):
    bsz, seq, d = x.shape
    n = bsz * seq
    xf = x.reshape(n, d)
    tm = _tile(n, 512)
    glu, q, k, v, sgc, sgn = _in_proj(xf, w["g_mix"], w["w_in"], w["b_in"], tm)
    ca = _conv_branch(glu.reshape(bsz, seq, CONV_CH), w["conv_w"], w["conv_b"], w["conv_ln_g"],
                      w["conv_ln_b"], _tile(seq, 512))
    b3 = lambda a: a.reshape(bsz, seq, NA_WIDTH)
    na = _natten(b3(q), b3(k), b3(v), w["na_tab"], _tile(seq // GRID_W, 8))
    x1, h2p, top_e, top_w, pos, cnt = _mix_route(
        xf, ca.reshape(n, CONV_CH), na.reshape(n, NA_WIDTH), sgc, sgn, w["w_conv_out"], w["w_attn_out"],
        w["w_out"], w["g_ffn"], w["w_router_t"], w["router_bias"], tm)
    n_blocks = n * TOP_K // EXPERT_BLOCK + N_EXPERTS
    dest, block_expert, n_used = _dispatch_plan(top_e, pos, cnt, n_blocks)
    xs = _sc_dispatch(h2p, dest, n_blocks * EXPERT_BLOCK)
    ys = _experts(xs, block_expert, n_used, w["w_exp_gu"], w["w_exp_down"])
    yg = _sc_combine(ys, dest)
    out = _tail(x1, yg, top_w.T, h2p, p.reshape(n, PLE_DIM), w["w_sh_gu"], w["w_sh_down"], w["g_ple"],
                w["w_ple_gate"], w["w_ple_proj"], w["g_final"], tm, final_norm)
    return out.reshape(bsz, seq, d)


def kernel(x_prompt, x_sample, p_prompt, p_sample, g_mix, w_in, b_in, conv_w, conv_b, conv_ln_g,
           conv_ln_b, w_conv_out, rpb, w_attn_out, w_out, g_ffn, w_router, router_bias, w_exp_gu,
           w_exp_down, w_sh_gu, w_sh_down, g_ple, w_ple_gate, w_ple_proj, g_final):
    depth = g_mix.shape[0]
    xp, xs = x_prompt, x_sample
    for i in range(depth):
        row = lambda a: a[i].reshape(1, -1).astype(F32)
        w = {
            "g_mix": row(g_mix), "w_in": w_in[i].astype(BF16), "b_in": row(b_in),
            "conv_w": conv_w[i].astype(F32), "conv_b": row(conv_b),
            "conv_ln_g": row(conv_ln_g), "conv_ln_b": row(conv_ln_b),
            "w_conv_out": w_conv_out[i].astype(BF16), "na_tab": _na_bias_table(rpb[i]),
            "w_attn_out": w_attn_out[i].astype(BF16), "w_out": w_out[i].astype(BF16),
            "g_ffn": row(g_ffn), "w_router_t": w_router[i].T.astype(F32),
            "router_bias": router_bias[i].reshape(-1, 1).astype(F32),
            "w_exp_gu": w_exp_gu[i].astype(BF16), "w_exp_down": w_exp_down[i].astype(BF16),
            "w_sh_gu": w_sh_gu[i].astype(BF16), "w_sh_down": w_sh_down[i].astype(BF16),
            "g_ple": row(g_ple), "w_ple_gate": w_ple_gate[i].astype(BF16),
            "w_ple_proj": w_ple_proj[i].astype(BF16), "g_final": g_final.reshape(1, -1).astype(F32),
        }
        xp = _encoder_group(xp, p_prompt[i], w, i == depth - 1)
        xs = _encoder_group(xs, p_sample[i], w, i == depth - 1)
    return (xp, xs)
```

```python
import functools

import jax
import jax.numpy as jnp
import numpy as np
from jax import lax
from jax.experimental import pallas as pl
from jax.experimental.pallas import tpu as pltpu
from jax.experimental.pallas import tpu_sc as plsc

D_MODEL = 1024
GRID_W = 64
CONV_CH = 512
CONV_K = 31
NA_HEADS = 16
NA_HEAD_DIM = 32
NA_WIDTH = NA_HEADS * NA_HEAD_DIM
WIN_R = 8
WIN_C = 16
N_EXPERTS = 64
TOP_K = 8
N_GROUP = 8
TOPK_GROUP = 4
EXPERT_FF = 256
SHARED_FF = 256
ROUTED_SCALE = 2.5
PLE_DIM = 256
EPS = 1e-6
NEG_BIG = -1e30

OFF_Q = 2 * CONV_CH
OFF_K = OFF_Q + NA_WIDTH
OFF_V = OFF_K + NA_WIDTH
OFF_GATE = OFF_V + NA_WIDTH

CONV_HALO = 16
CONV_ROWS = 32
VMEM_LIMIT = 56 * 1024 * 1024
LANES = 128
SUBLANES = 8
HEADS_PER_GROUP = LANES // NA_HEAD_DIM
NA_GROUPS = NA_HEADS // HEADS_PER_GROUP
HALF_D = D_MODEL // 2
HI_HALF_MASK = -65536
EXPERT_BLOCK = 1024
SC_CHUNK = 128

BF16 = jnp.bfloat16
F32 = jnp.float32


def _cparams(sem):
    return pltpu.CompilerParams(dimension_semantics=sem, vmem_limit_bytes=VMEM_LIMIT)


def _const_spec(shape):
    nd = len(shape)
    return pl.BlockSpec(shape, lambda *_: (0,) * nd)


def _rms(xf, g):
    return xf * lax.rsqrt(jnp.mean(xf * xf, axis=-1, keepdims=True) + EPS) * g


def _sigmoid(x):
    return 1.0 / (1.0 + jnp.exp(-x))


def _in_proj_kernel(x_ref, g_ref, w_ref, b_ref, glu_ref, q_ref, k_ref, v_ref, sgc_ref, sgn_ref):
    h = _rms(x_ref[...], g_ref[...]).astype(BF16)

    def proj(lo, hi):
        return jnp.dot(h, w_ref[:, lo:hi], preferred_element_type=F32) + b_ref[:, lo:hi]

    u = proj(0, CONV_CH)
    gate = proj(CONV_CH, 2 * CONV_CH)
    glu_ref[...] = (u * _sigmoid(gate)).astype(BF16)
    q_ref[...] = (proj(OFF_Q, OFF_K) * (NA_HEAD_DIM ** -0.5)).astype(BF16)
    k_ref[...] = proj(OFF_K, OFF_V).astype(BF16)
    v_ref[...] = proj(OFF_V, OFF_GATE).astype(BF16)
    for c in range(2):
        lo = OFF_GATE + c * CONV_CH
        sgc_ref[:, c * CONV_CH:(c + 1) * CONV_CH] = _sigmoid(proj(lo, lo + CONV_CH)).astype(BF16)
        lo = OFF_GATE + D_MODEL + c * CONV_CH
        sgn_ref[:, c * CONV_CH:(c + 1) * CONV_CH] = _sigmoid(proj(lo, lo + CONV_CH)).astype(BF16)


def _in_proj(x, g_mix, w_in, b_in, tm):
    n = x.shape[0]
    cols = w_in.shape[1]
    row = lambda w: pl.BlockSpec((tm, w), lambda i: (i, 0))
    out_w = (CONV_CH, NA_WIDTH, NA_WIDTH, NA_WIDTH, D_MODEL, D_MODEL)
    return pl.pallas_call(
        _in_proj_kernel,
        grid=(n // tm,),
        in_specs=[row(D_MODEL), _const_spec((1, D_MODEL)), _const_spec((D_MODEL, cols)),
                  _const_spec((1, cols))],
        out_specs=[row(w) for w in out_w],
        out_shape=[jax.ShapeDtypeStruct((n, w), BF16) for w in out_w],
        compiler_params=_cparams(("parallel",)),
        name="in_proj",
    )(x, g_mix, w_in, b_in)


def _conv_kernel(prev_ref, cur_ref, next_ref, w_ref, b_ref, lg_ref, lb_ref, o_ref, ext_ref, *, ts):
    i = pl.program_id(1)
    last = pl.num_programs(1) - 1
    prev = prev_ref[0].astype(F32)
    nxt = next_ref[0].astype(F32)
    ext_ref[0, 0:CONV_HALO, :] = jnp.where(i > 0, prev, 0.0)
    ext_ref[0, CONV_HALO:CONV_HALO + ts, :] = cur_ref[0].astype(F32)
    ext_ref[0, CONV_HALO + ts:, :] = jnp.where(i < last, nxt, 0.0)
    span = ts + 2 * CONV_HALO - SUBLANES
    for s in range(1, SUBLANES):
        ext_ref[s, 0:span, :] = ext_ref[0, s:s + span, :]
    base = CONV_HALO - CONV_K // 2
    for c0 in range(0, ts, CONV_ROWS):
        acc = jnp.zeros((CONV_ROWS, CONV_CH), F32)
        for j in range(CONV_K):
            shift = (base + j) % SUBLANES
            lo = base + j - shift + c0
            acc = acc + ext_ref[shift, lo:lo + CONV_ROWS, :] * w_ref[j:j + 1, :]
        y = acc + b_ref[...]
        mu = jnp.mean(y, axis=-1, keepdims=True)
        yc = y - mu
        var = jnp.mean(yc * yc, axis=-1, keepdims=True)
        z = yc * lax.rsqrt(var + EPS) * lg_ref[...] + lb_ref[...]
        o_ref[0, c0:c0 + CONV_ROWS, :] = (z * _sigmoid(z)).astype(BF16)


def _conv_branch(glu, conv_w, conv_b, ln_g, ln_b, ts):
    bsz, seq, ch = glu.shape
    nh = ts // CONV_HALO
    n_halo = seq // CONV_HALO
    return pl.pallas_call(
        functools.partial(_conv_kernel, ts=ts),
        grid=(bsz, seq // ts),
        in_specs=[
            pl.BlockSpec((1, CONV_HALO, ch), lambda b, i: (b, jnp.maximum(i * nh - 1, 0), 0)),
            pl.BlockSpec((1, ts, ch), lambda b, i: (b, i, 0)),
            pl.BlockSpec((1, CONV_HALO, ch), lambda b, i: (b, jnp.minimum((i + 1) * nh, n_halo - 1), 0)),
            _const_spec((CONV_K, ch)), _const_spec((1, ch)), _const_spec((1, ch)), _const_spec((1, ch)),
        ],
        out_specs=pl.BlockSpec((1, ts, ch), lambda b, i: (b, i, 0)),
        out_shape=jax.ShapeDtypeStruct((bsz, seq, ch), BF16),
        scratch_shapes=[pltpu.VMEM((SUBLANES, ts + 2 * CONV_HALO, ch), F32)],
        compiler_params=_cparams(("parallel", "parallel")),
        name="conv_branch",
    )(glu, glu, glu, conv_w, conv_b, ln_g, ln_b)


def _na_bias_table(rpb):
    col = np.arange(GRID_W)
    col_start = np.clip(col - WIN_C // 2, 0, GRID_W - WIN_C)
    kc = np.arange(GRID_W)
    inside = (kc[None, :] >= col_start[:, None]) & (kc[None, :] < col_start[:, None] + WIN_C)
    dc = kc[None, :] - col[:, None] + (WIN_C - 1)
    n_dc = 2 * WIN_C - 1
    onehot = (inside[:, :, None] & (dc[:, :, None] == np.arange(n_dc))).astype(np.float32)
    t = jnp.einsum("hrd,ckd->hrck", rpb.astype(F32), jnp.asarray(onehot),
                   precision=lax.Precision.HIGHEST)
    t = jnp.where(jnp.asarray(inside)[None, None], t, NEG_BIG)
    t = jnp.stack([t[:, o:o + WIN_R] for o in range(WIN_R)], axis=0)
    t = jnp.transpose(t, (0, 1, 3, 2, 4))
    return t.reshape(WIN_R, NA_GROUPS, HEADS_PER_GROUP * GRID_W, WIN_R * GRID_W)


def _na_kernel(q_ref, kp_ref, kc_ref, kn_ref, vp_ref, vc_ref, vn_ref, tab_ref, hmask_ref, o_ref,
               kband_ref, vaug_ref, s_ref, p_ref, *, rows, rb_rows):
    blk = rb_rows * GRID_W
    rb = pl.program_id(1)
    kband_ref[0:blk, :] = kp_ref[0]
    kband_ref[blk:2 * blk, :] = kc_ref[0]
    kband_ref[2 * blk:, :] = kn_ref[0]
    ones = jnp.ones((blk, LANES), BF16)
    for part, v_ref in enumerate((vp_ref, vc_ref, vn_ref)):
        rs = slice(part * blk, (part + 1) * blk)
        for g in range(NA_GROUPS):
            vaug_ref[rs, 2 * g * LANES:(2 * g + 1) * LANES] = v_ref[0, :, g * LANES:(g + 1) * LANES]
            vaug_ref[rs, (2 * g + 1) * LANES:(2 * g + 2) * LANES] = ones
    band = WIN_R * GRID_W
    lane_head = lax.broadcasted_iota(jnp.int32, (GRID_W, LANES), 1) // NA_HEAD_DIM

    def row_body(i, carry):
        r = rb * rb_rows + i
        r0 = jnp.clip(r - WIN_R // 2, 0, rows - WIN_R)
        start = pl.multiple_of((r0 - (rb - 1) * rb_rows) * GRID_W, GRID_W)
        o = r0 - r + (WIN_R - 1)
        qoff = pl.multiple_of(i * GRID_W, GRID_W)
        qi = q_ref[0, pl.ds(qoff, GRID_W), :]
        for g in range(NA_GROUPS):
            gl = slice(g * LANES, (g + 1) * LANES)
            q4 = jnp.concatenate([qi[:, gl]] * HEADS_PER_GROUP, axis=0) * hmask_ref[...]
            s = lax.dot_general(q4, kband_ref[pl.ds(start, band), gl], (((1,), (1,)), ((), ())),
                                preferred_element_type=F32)
            s_ref[g] = s + tab_ref[o, g]
        for g in range(NA_GROUPS):
            s = s_ref[g]
            m = jnp.max(s, axis=-1, keepdims=True)
            p_ref[g] = jnp.exp(s - m).astype(BF16)
        for g in range(NA_GROUPS):
            ol = jnp.dot(p_ref[g], vaug_ref[pl.ds(start, band), 2 * g * LANES:(2 * g + 2) * LANES],
                         preferred_element_type=F32)
            acc = jnp.zeros((GRID_W, LANES), F32)
            for h in range(HEADS_PER_GROUP):
                rs = slice(h * GRID_W, (h + 1) * GRID_W)
                acc = jnp.where(lane_head == h, ol[rs, :LANES] / ol[rs, LANES:], acc)
            o_ref[0, pl.ds(qoff, GRID_W), g * LANES:(g + 1) * LANES] = acc.astype(BF16)
        return carry

    lax.fori_loop(0, rb_rows, row_body, 0)


def _natten(q, k, v, tab, rb_rows):
    bsz, seq, width = q.shape
    rows = seq // GRID_W
    nrb = rows // rb_rows
    blk = rb_rows * GRID_W
    band = WIN_R * GRID_W
    stacked = HEADS_PER_GROUP * GRID_W
    hmask = (np.arange(stacked)[:, None] // GRID_W == np.arange(LANES)[None, :] // NA_HEAD_DIM)
    cur = pl.BlockSpec((1, blk, width), lambda b, i: (b, i, 0))
    prv = pl.BlockSpec((1, blk, width), lambda b, i: (b, jnp.maximum(i - 1, 0), 0))
    nxt = pl.BlockSpec((1, blk, width), lambda b, i: (b, jnp.minimum(i + 1, nrb - 1), 0))
    return pl.pallas_call(
        functools.partial(_na_kernel, rows=rows, rb_rows=rb_rows),
        grid=(bsz, nrb),
        in_specs=[cur, prv, cur, nxt, prv, cur, nxt,
                  pl.BlockSpec(tab.shape, lambda b, i: (0, 0, 0, 0), pipeline_mode=pl.Buffered(1)),
                  _const_spec(hmask.shape)],
        out_specs=cur,
        out_shape=jax.ShapeDtypeStruct((bsz, seq, width), BF16),
        scratch_shapes=[pltpu.VMEM((3 * blk, width), BF16), pltpu.VMEM((3 * blk, 2 * width), BF16),
                        pltpu.VMEM((NA_GROUPS, stacked, band), F32),
                        pltpu.VMEM((NA_GROUPS, stacked, band), BF16)],
        compiler_params=_cparams(("parallel", "parallel")),
        name="natten",
    )(q, k, k, k, v, v, v, tab, jnp.asarray(hmask, BF16))


def _pack_bf16_pair(lo, hi):
    lo_bits = lax.bitcast_convert_type(lo.astype(BF16).astype(F32), jnp.int32)
    hi_bits = lax.bitcast_convert_type(hi.astype(BF16).astype(F32), jnp.int32)
    return lax.shift_right_logical(lo_bits, 16) | (hi_bits & HI_HALF_MASK)


def _unpack_bf16_pair(word):
    lo = lax.bitcast_convert_type(lax.shift_left(word, 16), F32)
    hi = lax.bitcast_convert_type(word & HI_HALF_MASK, F32)
    return lo, hi


def _route(logits_t, bias_col):
    e, t = logits_t.shape
    per_group = e // N_GROUP
    scores = _sigmoid(logits_t)
    biased = scores + bias_col
    b3 = biased.reshape(N_GROUP, per_group, t)
    idx3 = lax.broadcasted_iota(jnp.int32, b3.shape, 1)
    m1 = jnp.max(b3, axis=1, keepdims=True)
    first = jnp.min(jnp.where(b3 == m1, idx3, per_group), axis=1, keepdims=True)
    m2 = jnp.max(jnp.where(idx3 == first, -jnp.inf, b3), axis=1, keepdims=True)
    gscore = (m1 + m2).reshape(N_GROUP, t)
    gidx = lax.broadcasted_iota(jnp.int32, gscore.shape, 0)
    grank = jnp.zeros(gscore.shape, jnp.int32)
    for g in range(N_GROUP):
        other = gscore[g:g + 1, :]
        beats = (other > gscore) | ((other == gscore) & (g < gidx))
        grank = grank + beats.astype(jnp.int32)
    gsel = (grank < TOPK_GROUP).reshape(N_GROUP, 1, t)
    masked = jnp.where(gsel, b3, -jnp.inf).reshape(e, t)
    sub = 8
    blocks = [masked[r:r + sub, :] for r in range(0, e, sub)]
    ranks = [jnp.zeros((sub, t), F32) for _ in blocks]
    row_in_block = lax.broadcasted_iota(jnp.int32, (sub, t), 0)
    for j in range(e):
        other = masked[j:j + 1, :]
        for bi, blk in enumerate(blocks):
            r0 = bi * sub
            if r0 > j:
                beats = other >= blk
            elif r0 + sub - 1 <= j:
                beats = other > blk
            else:
                beats = (other > blk) | ((other == blk) & (row_in_block > j - r0))
            ranks[bi] = ranks[bi] + jnp.where(beats, 1.0, 0.0)
    sel = jnp.concatenate(ranks, axis=0) < float(TOP_K)
    w = jnp.where(sel, scores, 0.0)
    return sel, w / jnp.sum(w, axis=0, keepdims=True) * ROUTED_SCALE


def _mix_kernel(x_ref, ca_ref, na_ref, sgc_ref, sgn_ref, wco_ref, wao_ref, wout_ref, g_ref,
                wr_ref, rb_ref, tri_ref, x1_ref, h2p_ref, te_ref, tw_ref, pos_ref, cnt_ref, base_ref):
    @pl.when(pl.program_id(0) == 0)
    def _():
        base_ref[...] = jnp.zeros_like(base_ref)

    yc = jnp.dot(ca_ref[...], wco_ref[...], preferred_element_type=F32)
    yn = jnp.dot(na_ref[...], wao_ref[...], preferred_element_type=F32)
    m = sgc_ref[...].astype(F32) * yc + sgn_ref[...].astype(F32) * yn
    x1 = x_ref[...] + jnp.dot(m.astype(BF16), wout_ref[...], preferred_element_type=F32)
    x1_ref[...] = x1
    h2 = _rms(x1, g_ref[...])
    h2p_ref[...] = _pack_bf16_pair(h2[:, :HALF_D], h2[:, HALF_D:])
    logits_t = lax.dot_general(wr_ref[...], h2, (((1,), (1,)), ((), ())),
                               preferred_element_type=F32, precision=lax.Precision.HIGHEST)
    sel, w = _route(logits_t, rb_ref[...])
    sel_b = jnp.where(sel, 1.0, 0.0).astype(BF16)
    pos = base_ref[:, 0:1] + jnp.dot(sel_b, tri_ref[...], preferred_element_type=F32)
    base_ref[...] = base_ref[...] + jnp.sum(sel_b.astype(F32), axis=1, keepdims=True)
    cnt_ref[...] = base_ref[...].astype(jnp.int32)
    ne = sel_b.shape[0]
    lower = jnp.where(lax.broadcasted_iota(jnp.int32, (ne, ne), 0) > lax.broadcasted_iota(jnp.int32, (ne, ne), 1),
                      1.0, 0.0).astype(BF16)
    slot = jnp.dot(lower, sel_b, preferred_element_type=F32)
    eidx = lax.broadcasted_iota(jnp.int32, sel.shape, 0).astype(F32)
    rows_e, rows_w, rows_p = [], [], []
    for k in range(TOP_K):
        mk = sel & (slot == float(k))
        pick = lambda a: jnp.sum(jnp.where(mk, a, 0.0), axis=0, keepdims=True)
        rows_e.append(pick(eidx))
        rows_w.append(pick(w))
        rows_p.append(pick(pos))
    te_ref[...] = jnp.concatenate(rows_e, axis=0).astype(jnp.int32)
    tw_ref[...] = jnp.concatenate(rows_w, axis=0)
    pos_ref[...] = jnp.concatenate(rows_p, axis=0).astype(jnp.int32)


def _mix_route(x, ca, na, sgc, sgn, wco, wao, wout, g_ffn, w_router_t, router_bias, tm):
    n = x.shape[0]
    row = lambda w: pl.BlockSpec((tm, w), lambda i: (i, 0))
    slots = pl.BlockSpec((TOP_K, tm), lambda i: (0, i))
    tri = (np.arange(tm)[:, None] < np.arange(tm)[None, :]).astype(np.float32)
    return pl.pallas_call(
        _mix_kernel,
        grid=(n // tm,),
        in_specs=[row(D_MODEL), row(CONV_CH), row(NA_WIDTH), row(D_MODEL), row(D_MODEL),
                  _const_spec(wco.shape), _const_spec(wao.shape), _const_spec(wout.shape),
                  _const_spec((1, D_MODEL)), _const_spec(w_router_t.shape), _const_spec((N_EXPERTS, 1)),
                  _const_spec((tm, tm))],
        out_specs=[row(D_MODEL), row(HALF_D), slots, slots, slots, _const_spec((N_EXPERTS, LANES))],
        out_shape=[jax.ShapeDtypeStruct((n, D_MODEL), F32), jax.ShapeDtypeStruct((n, HALF_D), jnp.int32),
                   jax.ShapeDtypeStruct((TOP_K, n), jnp.int32), jax.ShapeDtypeStruct((TOP_K, n), F32),
                   jax.ShapeDtypeStruct((TOP_K, n), jnp.int32),
                   jax.ShapeDtypeStruct((N_EXPERTS, LANES), jnp.int32)],
        scratch_shapes=[pltpu.VMEM((N_EXPERTS, LANES), F32)],
        compiler_params=_cparams(("arbitrary",)),
        name="mix_route",
    )(x, ca, na, sgc, sgn, wco, wao, wout, g_ffn, w_router_t, router_bias, jnp.asarray(tri, BF16))


def _dispatch_plan(top_e, pos, cnt, n_blocks):
    counts = cnt[:, 0]
    padded = ((counts + EXPERT_BLOCK - 1) // EXPERT_BLOCK) * EXPERT_BLOCK
    ends = jnp.cumsum(padded)
    starts = ends - padded
    experts = jnp.arange(N_EXPERTS, dtype=jnp.int32)
    start_of = jnp.sum(jnp.where(top_e[None] == experts[:, None, None], starts[:, None, None], 0), axis=0)
    dest = start_of + pos
    block_first_row = jnp.arange(n_blocks, dtype=jnp.int32) * EXPERT_BLOCK
    block_expert = jnp.minimum(jnp.sum(ends[None, :] <= block_first_row[:, None], axis=1),
                               N_EXPERTS - 1).astype(jnp.int32)
    n_used = (ends[-1:] // EXPERT_BLOCK).astype(jnp.int32)
    return dest.astype(jnp.int32), block_expert, n_used


def _sc_mesh():
    return plsc.VectorSubcoreMesh(core_axis_name="core", subcore_axis_name="subcore")


def _sc_worker_id(info):
    return lax.axis_index("subcore") * info.num_cores + lax.axis_index("core")


def _sc_dispatch(h2p, dest, n_rows):
    n, width = h2p.shape
    info = plsc.get_sparse_core_info()
    n_chunks = n // (info.num_cores * info.num_subcores * SC_CHUNK)
    dest3 = jnp.transpose(dest.reshape(TOP_K, n // SC_CHUNK, SC_CHUNK), (1, 0, 2))

    @functools.partial(
        pl.kernel, mesh=_sc_mesh(),
        out_type=jax.ShapeDtypeStruct((n_rows, width), h2p.dtype),
        scratch_types=[pltpu.VMEM((TOP_K, SC_CHUNK), jnp.int32), pltpu.VMEM((SC_CHUNK, width), h2p.dtype),
                       pltpu.SemaphoreType.DMA],
        name="moe_dispatch",
    )
    def dispatch(h_hbm, dest_hbm, xs_hbm, idx_v, rows_v, sem):
        wid = _sc_worker_id(info)

        @pl.loop(0, n_chunks)
        def _(i):
            c = wid * n_chunks + i
            pltpu.sync_copy(dest_hbm.at[c], idx_v)
            pltpu.sync_copy(h_hbm.at[pl.ds(c * SC_CHUNK, SC_CHUNK)], rows_v)
            copies = [pltpu.async_copy(rows_v, xs_hbm.at[idx_v.at[k]], sem) for k in range(TOP_K)]
            for cp in copies:
                cp.wait()

    return dispatch(h2p, dest3)


def _sc_combine(ys, dest):
    width = ys.shape[1]
    n_idx = dest.size
    info = plsc.get_sparse_core_info()
    per_worker = n_idx // (info.num_cores * info.num_subcores)
    n_chunks = per_worker // SC_CHUNK

    @functools.partial(
        pl.kernel, mesh=_sc_mesh(),
        out_type=jax.ShapeDtypeStruct((n_idx, width), ys.dtype),
        scratch_types=[pltpu.VMEM((SC_CHUNK,), jnp.int32), pltpu.VMEM((SC_CHUNK, width), ys.dtype),
                       pltpu.SemaphoreType.DMA],
        name="moe_combine",
    )
    def combine(ys_hbm, idx_hbm, yg_hbm, idx_v, rows_v, sem):
        base = _sc_worker_id(info) * per_worker

        @pl.loop(0, n_chunks)
        def _(i):
            off = base + i * SC_CHUNK
            pltpu.sync_copy(idx_hbm.at[pl.ds(off, SC_CHUNK)], idx_v)
            pltpu.async_copy(ys_hbm.at[idx_v], rows_v, sem).wait()
            pltpu.sync_copy(rows_v, yg_hbm.at[pl.ds(off, SC_CHUNK)])

    return combine(ys, dest.reshape(-1)).reshape(dest.shape + (width,))


def _experts_kernel(be_ref, nu_ref, xs_ref, wgu_ref, wd_ref, ys_ref):
    b = pl.program_id(0)

    @pl.when(b < nu_ref[0])
    def _():
        lo, hi = _unpack_bf16_pair(xs_ref[...])
        gu = (jnp.dot(lo.astype(BF16), wgu_ref[0, :HALF_D, :], preferred_element_type=F32)
              + jnp.dot(hi.astype(BF16), wgu_ref[0, HALF_D:, :], preferred_element_type=F32))
        g = gu[:, :EXPERT_FF]
        act = g * _sigmoid(g) * gu[:, EXPERT_FF:]
        y = jnp.dot(act.astype(BF16), wd_ref[0], preferred_element_type=F32)
        ys_ref[...] = _pack_bf16_pair(y[:, :HALF_D], y[:, HALF_D:])

    @pl.when(b >= nu_ref[0])
    def _():
        ys_ref[...] = jnp.zeros_like(ys_ref)


def _experts(xs, block_expert, n_used, wgu, wd):
    n_rows, width = xs.shape
    rows = pl.BlockSpec((EXPERT_BLOCK, width), lambda b, be, nu: (b, 0))
    return pl.pallas_call(
        _experts_kernel,
        grid_spec=pltpu.PrefetchScalarGridSpec(
            num_scalar_prefetch=2,
            grid=(n_rows // EXPERT_BLOCK,),
            in_specs=[rows,
                      pl.BlockSpec((1,) + wgu.shape[1:], lambda b, be, nu: (be[b], 0, 0)),
                      pl.BlockSpec((1,) + wd.shape[1:], lambda b, be, nu: (be[b], 0, 0))],
            out_specs=rows,
        ),
        out_shape=jax.ShapeDtypeStruct((n_rows, width), jnp.int32),
        compiler_params=_cparams(("parallel",)),
        name="moe_experts",
    )(block_expert, n_used, xs, wgu, wd)


def _tail_kernel(x1_ref, yg_ref, tw_ref, h2p_ref, p_ref, wsg_ref, wsd_ref, gp_ref, wpg_ref, wpp_ref, gf_ref,
                 o_ref, *, final_norm):
    ylo = yhi = None
    for k in range(TOP_K):
        lo, hi = _unpack_bf16_pair(yg_ref[k])
        wk = tw_ref[:, k:k + 1]
        ylo = wk * lo if ylo is None else ylo + wk * lo
        yhi = wk * hi if yhi is None else yhi + wk * hi
    hlo, hhi = _unpack_bf16_pair(h2p_ref[...])
    gu = (jnp.dot(hlo.astype(BF16), wsg_ref[:HALF_D, :], preferred_element_type=F32)
          + jnp.dot(hhi.astype(BF16), wsg_ref[HALF_D:, :], preferred_element_type=F32))
    g = gu[:, :SHARED_FF]
    act = g * _sigmoid(g) * gu[:, SHARED_FF:]
    sh = jnp.dot(act.astype(BF16), wsd_ref[...], preferred_element_type=F32)
    x2 = x1_ref[...] + jnp.concatenate([ylo, yhi], axis=-1) + sh
    h3 = _rms(x2, gp_ref[...]).astype(BF16)
    gate = _sigmoid(jnp.dot(h3, wpg_ref[...], preferred_element_type=F32))
    pp = jnp.dot(p_ref[...].astype(BF16), wpp_ref[...], preferred_element_type=F32)
    x3 = x2 + pp * gate
    o_ref[...] = _rms(x3, gf_ref[...]) if final_norm else x3


def _tail(x1, yg, tw, h2p, p, wsg, wsd, g_ple, wpg, wpp, g_final, tm, final_norm):
    n = x1.shape[0]
    row = lambda w: pl.BlockSpec((tm, w), lambda i: (i, 0))
    return pl.pallas_call(
        functools.partial(_tail_kernel, final_norm=final_norm),
        grid=(n // tm,),
        in_specs=[row(D_MODEL), pl.BlockSpec((TOP_K, tm, HALF_D), lambda i: (0, i, 0)), row(TOP_K),
                  row(HALF_D), row(PLE_DIM),
                  _const_spec(wsg.shape), _const_spec(wsd.shape), _const_spec((1, D_MODEL)),
                  _const_spec(wpg.shape), _const_spec(wpp.shape), _const_spec((1, D_MODEL))],
        out_specs=row(D_MODEL),
        out_shape=jax.ShapeDtypeStruct((n, D_MODEL), F32),
        compiler_params=_cparams(("parallel",)),
        name="tail",
    )(x1, yg, tw, h2p, p, wsg, wsd, g_ple, wpg, wpp, g_final)


def _tile(n, pref):
    t = min(pref, n)
    while n % t:
        t //= 2
    return t


def _encoder_group(x, p, w, final_norm):
    bsz, seq, d = x.shape
    n = bsz * seq
    xf = x.reshape(n, d)
    tm = _tile(n, 512)
    glu, q, k, v, sgc, sgn = _in_proj(xf, w["g_mix"], w["w_in"], w["b_in"], tm)
    ca = _conv_branch(glu.reshape(bsz, seq, CONV_CH), w["conv_w"], w["conv_b"], w["conv_ln_g"],
                      w["conv_ln_b"], _tile(seq, 512))
    b3 = lambda a: a.reshape(bsz, seq, NA_WIDTH)
    na = _natten(b3(q), b3(k), b3(v), w["na_tab"], _tile(seq // GRID_W, 8))
    x1, h2p, top_e, top_w, pos, cnt = _mix_route(
        xf, ca.reshape(n, CONV_CH), na.reshape(n, NA_WIDTH), sgc, sgn, w["w_conv_out"], w["w_attn_out"],
        w["w_out"], w["g_ffn"], w["w_router_t"], w["router_bias"], tm)
    n_blocks = n * TOP_K // EXPERT_BLOCK + N_EXPERTS
    dest, block_expert, n_used = _dispatch_plan(top_e, pos, cnt, n_blocks)
    xs = _sc_dispatch(h2p, dest, n_blocks * EXPERT_BLOCK)
    ys = _experts(xs, block_expert, n_used, w["w_exp_gu"], w["w_exp_down"])
    yg = _sc_combine(ys, dest)
    out = _tail(x1, yg, top_w.T, h2p, p.reshape(n, PLE_DIM), w["w_sh_gu"], w["w_sh_down"], w["g_ple"],
                w["w_ple_gate"], w["w_ple_proj"], w["g_final"], tm, final_norm)
    return out.reshape(bsz, seq, d)


def kernel(x_prompt, x_sample, p_prompt, p_sample, g_mix, w_in, b_in, conv_w, conv_b, conv_ln_g,
           conv_ln_b, w_conv_out, rpb, w_attn_out, w_out, g_ffn, w_router, router_bias, w_exp_gu,
           w_exp_down, w_sh_gu, w_sh_down, g_ple, w_ple_gate, w_ple_proj, g_final):
    depth = g_mix.shape[0]
    xp, xs = x_prompt, x_sample
    for i in range(depth):
        row = lambda a: a[i].reshape(1, -1).astype(F32)
        w = {
            "g_mix": row(g_mix), "w_in": w_in[i].astype(BF16), "b_in": row(b_in),
            "conv_w": conv_w[i].astype(F32), "conv_b": row(conv_b),
            "conv_ln_g": row(conv_ln_g), "conv_ln_b": row(conv_ln_b),
            "w_conv_out": w_conv_out[i].astype(BF16), "na_tab": _na_bias_table(rpb[i]),
            "w_attn_out": w_attn_out[i].astype(BF16), "w_out": w_out[i].astype(BF16),
            "g_ffn": row(g_ffn), "w_router_t": w_router[i].T.astype(F32),
            "router_bias": router_bias[i].reshape(-1, 1).astype(F32),
            "w_exp_gu": w_exp_gu[i].astype(BF16), "w_exp_down": w_exp_down[i].astype(BF16),
            "w_sh_gu": w_sh_gu[i].astype(BF16), "w_sh_down": w_sh_down[i].astype(BF16),
            "g_ple": row(g_ple), "w_ple_gate": w_ple_gate[i].astype(BF16),
            "w_ple_proj": w_ple_proj[i].astype(BF16), "g_final": g_final.reshape(1, -1).astype(F32),
        }
        xp = _encoder_group(xp, p_prompt[i], w, i == depth - 1)
        xs = _encoder_group(xs, p_sample[i], w, i == depth - 1)
    return (xp, xs)
```
